```python
import jax, jax.numpy as jnp
from jax import lax
import numpy as np

D_MODEL = 1024
BATCH = 16
SEQ = 2048
DEPTH = 4

N_A_LAYERS = DEPTH // 2
N_B_LAYERS = DEPTH - N_A_LAYERS

GLA_HEADS = 4
GLA_DK = D_MODEL // 2 // GLA_HEADS
GLA_DV = D_MODEL // GLA_HEADS
GLA_QK = GLA_HEADS * GLA_DK
GLA_V = GLA_HEADS * GLA_DV
GLA_RANK = 16
GLA_TAU = 16.0
GLA_CHUNK = 64
GLA_IN = 2 * GLA_QK + 2 * GLA_V + GLA_RANK

FOX_HEAD_DIM = 64
FOX_HEADS = D_MODEL // FOX_HEAD_DIM
Q_BLOCK = 128
KV_OUT = 2 * D_MODEL + FOX_HEADS

D_FF = -(-8 * D_MODEL // (3 * 256)) * 256
EPS = 1e-6

kernel_name = "yoco_gla_fox_hybrid"


def rmsnorm(x, g):
    xf = x.astype(jnp.float32)
    y = xf * lax.rsqrt(jnp.mean(xf * xf, axis=-1, keepdims=True) + EPS) * g.astype(jnp.float32)
    return y.astype(x.dtype)


def swiglu(h, w_gu, w_down):
    gate, up = jnp.split(h @ w_gu, 2, axis=-1)
    return (jax.nn.silu(gate) * up) @ w_down


def gla_mixer(h, w_in, w_alpha_up, b_alpha, g_head, w_out):
    B, T, _ = h.shape
    NC, C = T // GLA_CHUNK, GLA_CHUNK
    proj = h @ w_in
    q, k, v, r, a_low = jnp.split(
        proj, [GLA_QK, 2 * GLA_QK, 2 * GLA_QK + GLA_V, 2 * GLA_QK + 2 * GLA_V], axis=-1)
    log_alpha = jax.nn.log_sigmoid((a_low @ w_alpha_up + b_alpha).astype(jnp.float32)) / GLA_TAU

    def heads(t, d):
        return t.astype(jnp.float32).reshape(B, NC, C, GLA_HEADS, d).transpose(0, 3, 1, 2, 4)

    q = heads(q, GLA_DK) * (GLA_DK ** -0.5)
    k = heads(k, GLA_DK)
    v = heads(v, GLA_DV)
    b = jnp.cumsum(heads(log_alpha, GLA_DK), axis=3)
    b_last = b[:, :, :, -1:, :]
    q_dec = q * jnp.exp(b)
    k_dec = k * jnp.exp(-b)
    causal = jnp.tril(jnp.ones((C, C), dtype=bool))
    A = jnp.where(causal, jnp.einsum('bhncd,bhnsd->bhncs', q_dec, k_dec), 0.0)
    o_intra = jnp.einsum('bhncs,bhnsv->bhncv', A, v)
    chunk_kv = jnp.einsum('bhncd,bhncv->bhndv', k * jnp.exp(b_last - b), v)
    chunk_decay = jnp.exp(b_last[:, :, :, 0, :])

    def step(S, inp):
        decay, kv = inp
        return decay[..., None] * S + kv, S

    S0 = jnp.zeros((B, GLA_HEADS, GLA_DK, GLA_DV), jnp.float32)
    _, S_prev = lax.scan(step, S0, (jnp.moveaxis(chunk_decay, 2, 0), jnp.moveaxis(chunk_kv, 2, 0)))
    S_prev = jnp.moveaxis(S_prev, 0, 2)
    o = o_intra + jnp.einsum('bhncd,bhndv->bhncv', q_dec, S_prev)
    o = o.transpose(0, 2, 3, 1, 4).reshape(B, T, GLA_HEADS, GLA_DV)
    o = rmsnorm(o, g_head).reshape(B, T, GLA_V)
    o = o * jax.nn.silu(r.astype(jnp.float32))
    return o.astype(h.dtype) @ w_out


def shared_kv(h, kv_norm, w_kv, b_f):
    B, T, _ = h.shape
    proj = rmsnorm(h, kv_norm) @ w_kv
    k, v, f_logit = jnp.split(proj, [D_MODEL, 2 * D_MODEL], axis=-1)
    k = k.reshape(B, T, FOX_HEADS, FOX_HEAD_DIM).transpose(0, 2, 1, 3)
    v = v.reshape(B, T, FOX_HEADS, FOX_HEAD_DIM).transpose(0, 2, 1, 3)
    log_f = jax.nn.log_sigmoid(f_logit.astype(jnp.float32) + b_f.astype(jnp.float32))
    c = jnp.cumsum(log_f, axis=1).transpose(0, 2, 1)
    return k, v, c


def fox_mixer(h, k, v, c, w_q, w_o):
    B, T, _ = h.shape
    q = (h @ w_q).reshape(B, T, FOX_HEADS, FOX_HEAD_DIM).transpose(0, 2, 1, 3)
    q = q * (FOX_HEAD_DIM ** -0.5)
    outs = []
    for blk in range(T // Q_BLOCK):
        s0, e = blk * Q_BLOCK, (blk + 1) * Q_BLOCK
        logits = jnp.einsum('bhqd,bhkd->bhqk', q[:, :, s0:e], k[:, :, :e]).astype(jnp.float32)
        logits = logits + c[:, :, s0:e, None] - c[:, :, None, :e]
        mask = (s0 + jnp.arange(Q_BLOCK))[:, None] >= jnp.arange(e)[None, :]
        p = jax.nn.softmax(jnp.where(mask, logits, -jnp.inf), axis=-1).astype(v.dtype)
        outs.append(jnp.einsum('bhqk,bhkd->bhqd', p, v[:, :, :e]))
    o = jnp.concatenate(outs, axis=2).transpose(0, 2, 1, 3).reshape(B, T, D_MODEL)
    return o @ w_o


def setup_inputs(seed: int = 0) -> dict:
    key = jax.random.key(seed)
    ks = jax.random.split(key, 20)
    f32 = jnp.float32
    D = D_MODEL
    res_scale = (2.0 * DEPTH) ** -0.5

    def nrm(k, shape, scale):
        return jax.random.normal(k, shape, f32) * scale

    def gain(k, shape):
        return 1.0 + 0.01 * jax.random.normal(k, shape, f32)

    x = jax.random.normal(ks[0], (BATCH, SEQ, D), f32)
    attn_norm = gain(ks[1], (DEPTH, D))
    ffn_norm = gain(ks[2], (DEPTH, D))
    gla_w_in = nrm(ks[3], (N_A_LAYERS, D, GLA_IN), D ** -0.5)
    gla_w_alpha_up = nrm(ks[4], (N_A_LAYERS, GLA_RANK, GLA_QK), GLA_RANK ** -0.5)
    gla_b_alpha = nrm(ks[5], (N_A_LAYERS, GLA_QK), 0.1)
    gla_g_head = gain(ks[6], (N_A_LAYERS, GLA_DV))
    gla_w_out = nrm(ks[7], (N_A_LAYERS, GLA_V, D), GLA_V ** -0.5 * res_scale)
    kv_norm = gain(ks[8], (D,))
    w_kv = jnp.concatenate([
        nrm(ks[9], (D, 2 * D), D ** -0.5),
        nrm(ks[10], (D, FOX_HEADS), 0.1 * D ** -0.5)], axis=-1)
    b_f = 3.0 + nrm(ks[11], (FOX_HEADS,), 0.1)
    fox_w_q = nrm(ks[12], (N_B_LAYERS, D, D), D ** -0.5)
    fox_w_o = nrm(ks[13], (N_B_LAYERS, D, D), D ** -0.5 * res_scale)
    ffn_w_gu = nrm(ks[14], (DEPTH, D, 2 * D_FF), D ** -0.5)
    ffn_w_down = nrm(ks[15], (DEPTH, D_FF, D), D_FF ** -0.5 * res_scale)
    final_norm = gain(ks[16], (D,))
    return {"x": x, "attn_norm": attn_norm, "ffn_norm": ffn_norm,
            "gla_w_in": gla_w_in, "gla_w_alpha_up": gla_w_alpha_up, "gla_b_alpha": gla_b_alpha,
            "gla_g_head": gla_g_head, "gla_w_out": gla_w_out,
            "kv_norm": kv_norm, "w_kv": w_kv, "b_f": b_f,
            "fox_w_q": fox_w_q, "fox_w_o": fox_w_o,
            "ffn_w_gu": ffn_w_gu, "ffn_w_down": ffn_w_down, "final_norm": final_norm}


def reference(x, attn_norm, ffn_norm, gla_w_in, gla_w_alpha_up, gla_b_alpha, gla_g_head,
              gla_w_out, kv_norm, w_kv, b_f, fox_w_q, fox_w_o, ffn_w_gu, ffn_w_down, final_norm):
    h = x
    k_sh = v_sh = c_sh = None
    for layer in range(DEPTH):
        if layer < N_A_LAYERS:
            h = h + gla_mixer(rmsnorm(h, attn_norm[layer]), gla_w_in[layer], gla_w_alpha_up[layer],
                              gla_b_alpha[layer], gla_g_head[layer], gla_w_out[layer])
        else:
            if layer == N_A_LAYERS:
                k_sh, v_sh, c_sh = shared_kv(h, kv_norm, w_kv, b_f)
            j = layer - N_A_LAYERS
            h = h + fox_mixer(rmsnorm(h, attn_norm[layer]), k_sh, v_sh, c_sh, fox_w_q[j], fox_w_o[j])
        h = h + swiglu(rmsnorm(h, ffn_norm[layer]), ffn_w_gu[layer], ffn_w_down[layer])
    return rmsnorm(h, final_norm)
```

```python
import functools

import jax
import jax.numpy as jnp
from jax import lax
from jax.experimental import pallas as pl
from jax.experimental.pallas import tpu as pltpu

F32 = jnp.float32
BF16 = jnp.bfloat16
HIGHEST = lax.Precision.HIGHEST

EPS = 1e-6
GLA_HEADS = 4
GLA_RANK = 16
GLA_TAU = 16.0
GLA_CHUNK = 64
FOX_HEAD_DIM = 64

LANES = 128
MIB = 1024 * 1024

ROW_TILE = 512
GLA_TIME_TILE = 512
KV_TILE = 512
FOX_TILE = 256
FFN_CHUNKS = 2
NEG_BIG = -1e30

_CONTRACT_LAST = (((1,), (1,)), ((), ()))
_CONTRACT_FIRST = (((0,), (0,)), ((), ()))


def _rmsnorm(x, g):
    return x * lax.rsqrt(jnp.mean(x * x, axis=-1, keepdims=True) + EPS) * g


def _log_sigmoid(z):
    return jnp.minimum(z, 0.0) - jnp.log(1.0 + jnp.exp(-jnp.abs(z)))


def _silu(x):
    return x * jax.nn.sigmoid(x)


def _dot(a, b):
    return jnp.dot(a, b, preferred_element_type=F32)


def _resident(shape):
    return pl.BlockSpec(shape, lambda *_: (0,) * len(shape), pipeline_mode=pl.Buffered(1))


def _gla_in_kernel(x_ref, g_ref, wq_ref, wk_ref, wv_ref, wr_ref, wa_ref, wup_ref, ba_ref,
                   q_ref, k_ref, v_ref, r_ref, la_ref):
    hn = _rmsnorm(x_ref[...], g_ref[...]).astype(BF16)
    q_ref[...] = _dot(hn, wq_ref[...]).astype(BF16)
    k_ref[...] = _dot(hn, wk_ref[...]).astype(BF16)
    v_ref[...] = _dot(hn, wv_ref[...]).astype(BF16)
    r_ref[...] = _dot(hn, wr_ref[...]).astype(BF16)
    a_low = _dot(hn, wa_ref[...])
    z = jnp.dot(a_low, wup_ref[...], precision=HIGHEST, preferred_element_type=F32) + ba_ref[...]
    la_ref[...] = _log_sigmoid(z) * (1.0 / GLA_TAU)


def _gla_in_proj(x2d, g, w_in, w_up, b_alpha):
    m, d = x2d.shape
    qk = w_up.shape[1]
    vdim = (w_in.shape[1] - 2 * qk - GLA_RANK) // 2
    wq = w_in[:, :qk].astype(BF16)
    wk = w_in[:, qk:2 * qk].astype(BF16)
    wv = w_in[:, 2 * qk:2 * qk + vdim].astype(BF16)
    wr = w_in[:, 2 * qk + vdim:2 * qk + 2 * vdim].astype(BF16)
    wa = jnp.pad(w_in[:, 2 * qk + 2 * vdim:], ((0, 0), (0, LANES - GLA_RANK))).astype(BF16)
    wup = jnp.pad(w_up, ((0, LANES - GLA_RANK), (0, 0)))
    row = lambda n: pl.BlockSpec((ROW_TILE, n), lambda i: (i, 0))
    return pl.pallas_call(
        _gla_in_kernel,
        grid=(m // ROW_TILE,),
        in_specs=[row(d), _resident((1, d)), _resident((d, qk)), _resident((d, qk)),
                  _resident((d, vdim)), _resident((d, vdim)), _resident((d, LANES)),
                  _resident((LANES, qk)), _resident((1, qk))],
        out_specs=[row(qk), row(qk), row(vdim), row(vdim), row(qk)],
        out_shape=[jax.ShapeDtypeStruct((m, qk), BF16), jax.ShapeDtypeStruct((m, qk), BF16),
                   jax.ShapeDtypeStruct((m, vdim), BF16), jax.ShapeDtypeStruct((m, vdim), BF16),
                   jax.ShapeDtypeStruct((m, qk), F32)],
        compiler_params=pltpu.CompilerParams(dimension_semantics=("parallel",),
                                             vmem_limit_bytes=40 * MIB),
        name="gla_in_proj",
    )(x2d, g.reshape(1, d), wq, wk, wv, wr, wa, wup, b_alpha.reshape(1, qk))


def _gla_kernel(q_ref, k_ref, v_ref, r_ref, la_ref, g_ref, o_ref, st_ref, *, dk, dv, n_chunks):
    c = GLA_CHUNK

    @pl.when(pl.program_id(1) == 0)
    def _():
        st_ref[...] = jnp.zeros_like(st_ref)

    row = lax.broadcasted_iota(jnp.int32, (c, c), 0)
    col = lax.broadcasted_iota(jnp.int32, (c, c), 1)
    causal = row >= col
    tri = causal.astype(F32)
    g = g_ref[...]
    scale = dk ** -0.5

    def chunk(n, carry):
        rows = pl.ds(pl.multiple_of(n * c, c), c)
        for h in range(GLA_HEADS):
            kcols = slice(h * dk, (h + 1) * dk)
            vcols = slice(h * dv, (h + 1) * dv)
            la = la_ref[rows, kcols]
            b = jnp.dot(tri, la, precision=HIGHEST, preferred_element_type=F32)
            b_last = b[c - 1:c, :]
            qf = q_ref[rows, kcols].astype(F32) * scale
            kf = k_ref[rows, kcols].astype(F32)
            q_dec = (qf * jnp.exp(b)).astype(BF16)
            k_dec = (kf * jnp.exp(-b)).astype(BF16)
            k_out = (kf * jnp.exp(b_last - b)).astype(BF16)
            v = v_ref[rows, vcols]
            a = lax.dot_general(q_dec, k_dec, _CONTRACT_LAST, preferred_element_type=F32)
            a = jnp.where(causal, a, 0.0).astype(BF16)
            st = st_ref[h]
            o = _dot(a, v) + lax.dot_general(q_dec, st.astype(BF16), _CONTRACT_LAST,
                                             preferred_element_type=F32)
            st_ref[h] = st * jnp.exp(b_last) + lax.dot_general(
                v, k_out, _CONTRACT_FIRST, preferred_element_type=F32)
            o = _rmsnorm(o, g)
            o_ref[rows, vcols] = (o * _silu(r_ref[rows, vcols].astype(F32))).astype(BF16)
        return carry

    lax.fori_loop(0, n_chunks, chunk, 0)


def _gla_recurrence(q, k, v, r, la, g_head, batch, seq):
    qk = q.shape[-1]
    vdim = v.shape[-1]
    dk, dv = qk // GLA_HEADS, vdim // GLA_HEADS
    tt = GLA_TIME_TILE
    shp = lambda a: a.reshape(batch, seq, a.shape[-1])
    spec = lambda n: pl.BlockSpec((None, tt, n), lambda b, t: (b, t, 0))
    out = pl.pallas_call(
        functools.partial(_gla_kernel, dk=dk, dv=dv, n_chunks=tt // GLA_CHUNK),
        grid=(batch, seq // tt),
        in_specs=[spec(qk), spec(qk), spec(vdim), spec(vdim), spec(qk), _resident((1, dv))],
        out_specs=spec(vdim),
        out_shape=jax.ShapeDtypeStruct((batch, seq, vdim), BF16),
        scratch_shapes=[pltpu.VMEM((GLA_HEADS, dv, dk), F32)],
        compiler_params=pltpu.CompilerParams(dimension_semantics=("parallel", "arbitrary"),
                                             vmem_limit_bytes=40 * MIB),
        name="gla_recurrence",
    )(shp(q), shp(k), shp(v), shp(r), shp(la), g_head.reshape(1, dv))
    return out.reshape(batch * seq, vdim)


def _out_ffn_kernel(x_ref, o_ref, wo_ref, g_ref, wg_ref, wu_ref, wd_ref, gf_ref, y_ref, *, final):
    x1 = x_ref[...] + _dot(o_ref[...], wo_ref[...])
    hn = _rmsnorm(x1, g_ref[...]).astype(BF16)
    dff = wg_ref.shape[1]
    fc = dff // FFN_CHUNKS
    acc = x1
    for ci in range(FFN_CHUNKS):
        cols = slice(ci * fc, (ci + 1) * fc)
        gate = _dot(hn, wg_ref[:, cols])
        up = _dot(hn, wu_ref[:, cols])
        acc = acc + _dot((_silu(gate) * up).astype(BF16), wd_ref[cols, :])
    if final:
        acc = _rmsnorm(acc, gf_ref[...])
    y_ref[...] = acc


def _out_ffn(x2d, o2d, w_o, g_ffn, w_gu, w_down, g_final, final):
    m, d = x2d.shape
    dff = w_down.shape[0]
    assert dff % (FFN_CHUNKS * LANES) == 0
    wg = w_gu[:, :dff].astype(BF16)
    wu = w_gu[:, dff:].astype(BF16)
    row = lambda: pl.BlockSpec((ROW_TILE, d), lambda i: (i, 0))
    return pl.pallas_call(
        functools.partial(_out_ffn_kernel, final=final),
        grid=(m // ROW_TILE,),
        in_specs=[row(), row(), _resident((d, d)), _resident((1, d)), _resident((d, dff)),
                  _resident((d, dff)), _resident((dff, d)), _resident((1, d))],
        out_specs=row(),
        out_shape=jax.ShapeDtypeStruct((m, d), F32),
        compiler_params=pltpu.CompilerParams(dimension_semantics=("parallel",),
                                             vmem_limit_bytes=52 * MIB),
        name="out_proj_ffn",
    )(x2d, o2d, w_o.astype(BF16), g_ffn.reshape(1, d), wg, wu, w_down.astype(BF16),
      g_final.reshape(1, d))


def _kv_kernel(x_ref, g_ref, wk_ref, wv_ref, wft_ref, bf_ref, k_ref, v_ref, c_ref, carry_ref):
    @pl.when(pl.program_id(1) == 0)
    def _():
        carry_ref[...] = jnp.zeros_like(carry_ref)

    hn = _rmsnorm(x_ref[...], g_ref[...]).astype(BF16)
    k_ref[...] = _dot(hn, wk_ref[...]).astype(BF16)
    v_ref[...] = _dot(hn, wv_ref[...]).astype(BF16)
    f_t = lax.dot_general(wft_ref[...], hn, _CONTRACT_LAST, preferred_element_type=F32)
    log_f = _log_sigmoid(f_t + bf_ref[...])
    n = log_f.shape[1]
    upper = (lax.broadcasted_iota(jnp.int32, (n, n), 0)
             <= lax.broadcasted_iota(jnp.int32, (n, n), 1)).astype(F32)
    c = jnp.dot(log_f, upper, precision=HIGHEST, preferred_element_type=F32) + carry_ref[:, 0:1]
    c_ref[...] = c
    carry_ref[...] = jnp.broadcast_to(c[:, n - 1:n], carry_ref.shape)


def _shared_kv(x3d, g, w_kv, b_f):
    batch, seq, d = x3d.shape
    heads = b_f.shape[0]
    wk = w_kv[:, :d].astype(BF16)
    wv = w_kv[:, d:2 * d].astype(BF16)
    wft = w_kv[:, 2 * d:].T.astype(BF16)
    tile = lambda n: pl.BlockSpec((None, KV_TILE, n), lambda b, t: (b, t, 0))
    return pl.pallas_call(
        _kv_kernel,
        grid=(batch, seq // KV_TILE),
        in_specs=[tile(d), _resident((1, d)), _resident((d, d)), _resident((d, d)),
                  _resident((heads, d)), _resident((heads, 1))],
        out_specs=[tile(d), tile(d), pl.BlockSpec((None, heads, KV_TILE), lambda b, t: (b, 0, t))],
        out_shape=[jax.ShapeDtypeStruct((batch, seq, d), BF16),
                   jax.ShapeDtypeStruct((batch, seq, d), BF16),
                   jax.ShapeDtypeStruct((batch, heads, seq), F32)],
        scratch_shapes=[pltpu.VMEM((heads, LANES), F32)],
        compiler_params=pltpu.CompilerParams(dimension_semantics=("parallel", "arbitrary"),
                                             vmem_limit_bytes=40 * MIB),
        name="shared_kv",
    )(x3d, g.reshape(1, d), wk, wv, wft, b_f.reshape(heads, 1))


def _q_kernel(x_ref, g_ref, wq_ref, q_ref):
    hn = _rmsnorm(x_ref[...], g_ref[...]).astype(BF16)
    q_ref[...] = (_dot(hn, wq_ref[...]) * (FOX_HEAD_DIM ** -0.5)).astype(BF16)


def _q_proj(x2d, g, w_q):
    m, d = x2d.shape
    row = lambda: pl.BlockSpec((ROW_TILE, d), lambda i: (i, 0))
    return pl.pallas_call(
        _q_kernel,
        grid=(m // ROW_TILE,),
        in_specs=[row(), _resident((1, d)), _resident((d, d))],
        out_specs=row(),
        out_shape=jax.ShapeDtypeStruct((m, d), BF16),
        compiler_params=pltpu.CompilerParams(dimension_semantics=("parallel",),
                                             vmem_limit_bytes=40 * MIB),
        name="fox_q_proj",
    )(x2d, g.reshape(1, d), w_q.astype(BF16))


def _fox_kernel(q_ref, k_ref, v_ref, ct_ref, cs_ref, o_ref):
    t = FOX_TILE
    hd = FOX_HEAD_DIM
    i = pl.program_id(2)
    q = q_ref[...]
    lane = lax.broadcasted_iota(jnp.int32, q.shape, 1)
    first = lane < hd
    zero = jnp.zeros_like(q)
    q_heads = (jnp.where(first, q, zero), jnp.where(first, zero, q))
    ct = ct_ref[...]
    diag = (lax.broadcasted_iota(jnp.int32, (t, t), 0)
            >= lax.broadcasted_iota(jnp.int32, (t, t), 1))

    def tile(j, carry, masked):
        cols = pl.ds(pl.multiple_of(j * t, t), t)
        ks = k_ref[cols, :]
        vs = v_ref[cols, :]
        cs = cs_ref[:, cols]
        new = []
        for h in range(2):
            m, l, acc = carry[h]
            s = lax.dot_general(q_heads[h], ks, _CONTRACT_LAST, preferred_element_type=F32)
            s = (s + ct[:, h:h + 1]) - cs[h:h + 1, :]
            if masked:
                s = jnp.where(diag, s, NEG_BIG)
            m_new = jnp.maximum(m, jnp.max(s, axis=1, keepdims=True))
            alpha = jnp.exp(m - m_new)
            p = jnp.exp(s - m_new)
            l = alpha * l + jnp.sum(p, axis=1, keepdims=True)
            acc = alpha * acc + _dot(p.astype(BF16), vs)
            new.append((m_new, l, acc))
        return tuple(new)

    init = tuple((jnp.full((t, 1), NEG_BIG, F32), jnp.zeros((t, 1), F32),
                  jnp.zeros((t, 2 * hd), F32)) for _ in range(2))
    carry = lax.fori_loop(0, i, lambda j, c: tile(j, c, False), init)
    (_, l0, a0), (_, l1, a1) = tile(i, carry, True)
    o_ref[...] = jnp.where(first, a0 / l0, a1 / l1).astype(BF16)


def _fox_attention(q, k, v, c, batch, seq):
    d = q.shape[-1]
    heads = c.shape[1]
    pairs = heads // 2
    t = FOX_TILE
    c_keys = c.reshape(batch, pairs, 2, seq)
    c_rows = c_keys.transpose(0, 1, 3, 2)
    q3 = q.reshape(batch, seq, d)
    out = pl.pallas_call(
        _fox_kernel,
        grid=(batch, pairs, seq // t),
        in_specs=[pl.BlockSpec((None, t, 2 * FOX_HEAD_DIM), lambda b, p, i: (b, i, p)),
                  pl.BlockSpec((None, seq, 2 * FOX_HEAD_DIM), lambda b, p, i: (b, 0, p)),
                  pl.BlockSpec((None, seq, 2 * FOX_HEAD_DIM), lambda b, p, i: (b, 0, p)),
                  pl.BlockSpec((None, None, t, 2), lambda b, p, i: (b, p, i, 0)),
                  pl.BlockSpec((None, None, 2, seq), lambda b, p, i: (b, p, 0, 0))],
        out_specs=pl.BlockSpec((None, t, 2 * FOX_HEAD_DIM), lambda b, p, i: (b, i, p)),
        out_shape=jax.ShapeDtypeStruct((batch, seq, d), BF16),
        compiler_params=pltpu.CompilerParams(
            dimension_semantics=("parallel", "parallel", "arbitrary"),
            vmem_limit_bytes=40 * MIB),
        name="fox_attention",
    )(q3, k, v, c_rows, c_keys)
    return out.reshape(batch * seq, d)


def kernel(x, attn_norm, ffn_norm, gla_w_in, gla_w_alpha_up, gla_b_alpha, gla_g_head, gla_w_out,
           kv_norm, w_kv, b_f, fox_w_q, fox_w_o, ffn_w_gu, ffn_w_down, final_norm):
    batch, seq, d = x.shape
    depth = attn_norm.shape[0]
    n_gla = gla_w_in.shape[0]
    assert (batch * seq) % ROW_TILE == 0 and seq % GLA_TIME_TILE == 0
    assert seq % KV_TILE == 0 and seq % FOX_TILE == 0 and 2 * FOX_HEAD_DIM == LANES

    h = x.reshape(batch * seq, d)
    k_sh = v_sh = c_sh = None
    for layer in range(depth):
        if layer < n_gla:
            q, k, v, r, la = _gla_in_proj(h, attn_norm[layer], gla_w_in[layer],
                                          gla_w_alpha_up[layer], gla_b_alpha[layer])
            o = _gla_recurrence(q, k, v, r, la, gla_g_head[layer], batch, seq)
            w_o = gla_w_out[layer]
        else:
            if layer == n_gla:
                k_sh, v_sh, c_sh = _shared_kv(h.reshape(batch, seq, d), kv_norm, w_kv, b_f)
            j = layer - n_gla
            q = _q_proj(h, attn_norm[layer], fox_w_q[j])
            o = _fox_attention(q, k_sh, v_sh, c_sh, batch, seq)
            w_o = fox_w_o[j]
        h = _out_ffn(h, o, w_o, ffn_norm[layer], ffn_w_gu[layer], ffn_w_down[layer],
                     final_norm, final=(layer == depth - 1))
    return h.reshape(batch, seq, d)
```

```python
import functools
import math

import jax
import jax.numpy as jnp
from jax import lax
from jax.experimental import pallas as pl
from jax.experimental.pallas import tpu as pltpu

F32 = jnp.float32
BF16 = jnp.bfloat16
HIGHEST = lax.Precision.HIGHEST

EPS = 1e-6
GLA_HEADS = 4
GLA_RANK = 16
GLA_TAU = 16.0
GLA_CHUNK = 64
FOX_HEAD_DIM = 64
LOG2E = math.log2(math.e)

LANES = 128
BF16_SUBLANES = 16
MIB = 1024 * 1024

ROW_TILE = 512
GLA_TIME_TILE = 512
KV_TILE = 512
FOX_TILE = 256
FFN_CHUNKS = 2
NEG_BIG = -1e30

_CONTRACT_LAST = (((1,), (1,)), ((), ()))
_CONTRACT_FIRST = (((0,), (0,)), ((), ()))


def _rmsnorm(x, g):
    return x * lax.rsqrt(jnp.mean(x * x, axis=-1, keepdims=True) + EPS) * g


def _log_sigmoid(z):
    return jnp.minimum(z, 0.0) - jnp.log(1.0 + jnp.exp(-jnp.abs(z)))


def _silu(x):
    return x * jax.nn.sigmoid(x)


def _dot(a, b):
    return jnp.dot(a, b, preferred_element_type=F32)


def _resident(shape):
    return pl.BlockSpec(shape, lambda *_: (0,) * len(shape), pipeline_mode=pl.Buffered(1))


def _gla_in_kernel(x_ref, g_ref, wq_ref, wk_ref, wv_ref, wr_ref, wa_ref, wup_ref, ba_ref,
                   q_ref, k_ref, v_ref, r_ref, la_ref):
    hn = _rmsnorm(x_ref[...], g_ref[...]).astype(BF16)
    q_ref[...] = _dot(hn, wq_ref[...]).astype(BF16)
    k_ref[...] = _dot(hn, wk_ref[...]).astype(BF16)
    v_ref[...] = _dot(hn, wv_ref[...]).astype(BF16)
    r_ref[...] = _dot(hn, wr_ref[...]).astype(BF16)
    a_low = _dot(hn, wa_ref[...])
    z = jnp.dot(a_low, wup_ref[...], precision=HIGHEST, preferred_element_type=F32) + ba_ref[...]
    la_ref[...] = _log_sigmoid(z) * (1.0 / GLA_TAU)


def _gla_in_proj(x2d, g, w_in, w_up, b_alpha):
    m, d = x2d.shape
    qk = w_up.shape[1]
    vdim = (w_in.shape[1] - 2 * qk - GLA_RANK) // 2
    wq = w_in[:, :qk].astype(BF16)
    wk = w_in[:, qk:2 * qk].astype(BF16)
    wv = w_in[:, 2 * qk:2 * qk + vdim].astype(BF16)
    wr = w_in[:, 2 * qk + vdim:2 * qk + 2 * vdim].astype(BF16)
    wa = jnp.pad(w_in[:, 2 * qk + 2 * vdim:], ((0, 0), (0, LANES - GLA_RANK))).astype(BF16)
    wup = jnp.pad(w_up, ((0, LANES - GLA_RANK), (0, 0)))
    row = lambda n: pl.BlockSpec((ROW_TILE, n), lambda i: (i, 0))
    return pl.pallas_call(
        _gla_in_kernel,
        grid=(m // ROW_TILE,),
        in_specs=[row(d), _resident((1, d)), _resident((d, qk)), _resident((d, qk)),
                  _resident((d, vdim)), _resident((d, vdim)), _resident((d, LANES)),
                  _resident((LANES, qk)), _resident((1, qk))],
        out_specs=[row(qk), row(qk), row(vdim), row(vdim), row(qk)],
        out_shape=[jax.ShapeDtypeStruct((m, qk), BF16), jax.ShapeDtypeStruct((m, qk), BF16),
                   jax.ShapeDtypeStruct((m, vdim), BF16), jax.ShapeDtypeStruct((m, vdim), BF16),
                   jax.ShapeDtypeStruct((m, qk), F32)],
        compiler_params=pltpu.CompilerParams(dimension_semantics=("parallel",),
                                             vmem_limit_bytes=40 * MIB),
        name="gla_in_proj",
    )(x2d, g.reshape(1, d), wq, wk, wv, wr, wa, wup, b_alpha.reshape(1, qk))


def _gla_kernel(q_ref, k_ref, v_ref, r_ref, la_ref, g_ref, o_ref, st_ref, *, dk, dv, n_chunks):
    c = GLA_CHUNK

    @pl.when(pl.program_id(1) == 0)
    def _():
        st_ref[...] = jnp.zeros_like(st_ref)

    row = lax.broadcasted_iota(jnp.int32, (c, c), 0)
    col = lax.broadcasted_iota(jnp.int32, (c, c), 1)
    causal = row >= col
    tri = causal.astype(F32)
    g = g_ref[...]
    scale = dk ** -0.5

    def chunk(n, carry):
        rows = pl.ds(pl.multiple_of(n * c, c), c)
        for h in range(GLA_HEADS):
            kcols = slice(h * dk, (h + 1) * dk)
            vcols = slice(h * dv, (h + 1) * dv)
            la = la_ref[rows, kcols]
            b = jnp.dot(tri, la, precision=HIGHEST, preferred_element_type=F32)
            b_last = b[c - 1:c, :]
            qf = q_ref[rows, kcols].astype(F32) * scale
            kf = k_ref[rows, kcols].astype(F32)
            q_dec = (qf * jnp.exp(b)).astype(BF16)
            k_dec = (kf * jnp.exp(-b)).astype(BF16)
            k_out = (kf * jnp.exp(b_last - b)).astype(BF16)
            v = v_ref[rows, vcols]
            a = lax.dot_general(q_dec, k_dec, _CONTRACT_LAST, preferred_element_type=F32)
            a = jnp.where(causal, a, 0.0).astype(BF16)
            st = st_ref[h]
            o = _dot(a, v) + lax.dot_general(q_dec, st.astype(BF16), _CONTRACT_LAST,
                                             preferred_element_type=F32)
            st_ref[h] = st * jnp.exp(b_last) + lax.dot_general(
                v, k_out, _CONTRACT_FIRST, preferred_element_type=F32)
            o = _rmsnorm(o, g)
            o_ref[rows, vcols] = (o * _silu(r_ref[rows, vcols].astype(F32))).astype(BF16)
        return carry

    lax.fori_loop(0, n_chunks, chunk, 0)


def _gla_recurrence(q, k, v, r, la, g_head, batch, seq):
    qk = q.shape[-1]
    vdim = v.shape[-1]
    dk, dv = qk // GLA_HEADS, vdim // GLA_HEADS
    tt = GLA_TIME_TILE
    shp = lambda a: a.reshape(batch, seq, a.shape[-1])
    spec = lambda n: pl.BlockSpec((None, tt, n), lambda b, t: (b, t, 0))
    out = pl.pallas_call(
        functools.partial(_gla_kernel, dk=dk, dv=dv, n_chunks=tt // GLA_CHUNK),
        grid=(batch, seq // tt),
        in_specs=[spec(qk), spec(qk), spec(vdim), spec(vdim), spec(qk), _resident((1, dv))],
        out_specs=spec(vdim),
        out_shape=jax.ShapeDtypeStruct((batch, seq, vdim), BF16),
        scratch_shapes=[pltpu.VMEM((GLA_HEADS, dv, dk), F32)],
        compiler_params=pltpu.CompilerParams(dimension_semantics=("parallel", "arbitrary"),
                                             vmem_limit_bytes=40 * MIB),
        name="gla_recurrence",
    )(shp(q), shp(k), shp(v), shp(r), shp(la), g_head.reshape(1, dv))
    return out.reshape(batch * seq, vdim)


def _out_ffn_kernel(x_ref, o_ref, wo_ref, g_ref, wg_ref, wu_ref, wd_ref, gf_ref, y_ref, *, final):
    x1 = x_ref[...] + _dot(o_ref[...], wo_ref[...])
    hn = _rmsnorm(x1, g_ref[...]).astype(BF16)
    dff = wg_ref.shape[1]
    fc = dff // FFN_CHUNKS
    acc = x1
    for ci in range(FFN_CHUNKS):
        cols = slice(ci * fc, (ci + 1) * fc)
        gate = _dot(hn, wg_ref[:, cols])
        up = _dot(hn, wu_ref[:, cols])
        acc = acc + _dot((_silu(gate) * up).astype(BF16), wd_ref[cols, :])
    if final:
        acc = _rmsnorm(acc, gf_ref[...])
    y_ref[...] = acc


def _out_ffn(x2d, o2d, w_o, g_ffn, w_gu, w_down, g_final, final):
    m, d = x2d.shape
    dff = w_down.shape[0]
    assert dff % (FFN_CHUNKS * LANES) == 0
    wg = w_gu[:, :dff].astype(BF16)
    wu = w_gu[:, dff:].astype(BF16)
    row = lambda: pl.BlockSpec((ROW_TILE, d), lambda i: (i, 0))
    return pl.pallas_call(
        functools.partial(_out_ffn_kernel, final=final),
        grid=(m // ROW_TILE,),
        in_specs=[row(), row(), _resident((d, d)), _resident((1, d)), _resident((d, dff)),
                  _resident((d, dff)), _resident((dff, d)), _resident((1, d))],
        out_specs=row(),
        out_shape=jax.ShapeDtypeStruct((m, d), F32),
        compiler_params=pltpu.CompilerParams(dimension_semantics=("parallel",),
                                             vmem_limit_bytes=52 * MIB),
        name="out_proj_ffn",
    )(x2d, o2d, w_o.astype(BF16), g_ffn.reshape(1, d), wg, wu, w_down.astype(BF16),
      g_final.reshape(1, d))


AUG_QSIDE_C = FOX_HEAD_DIM
AUG_KSIDE_C = FOX_HEAD_DIM + 3
AUG_PARTS = 3


def _split3(x):
    hi = x.astype(BF16)
    r1 = x - hi.astype(F32)
    mid = r1.astype(BF16)
    lo = (r1 - mid.astype(F32)).astype(BF16)
    return hi, mid, lo


def _kv_kernel(x_ref, g_ref, wk_ref, wvt_ref, wft_ref, bf_ref, place_ref,
               k_ref, vt_ref, c_ref, carry_ref):
    @pl.when(pl.program_id(1) == 0)
    def _():
        carry_ref[...] = jnp.zeros_like(carry_ref)

    hn = _rmsnorm(x_ref[...], g_ref[...]).astype(BF16)
    vt_ref[...] = lax.dot_general(wvt_ref[...], hn, _CONTRACT_LAST,
                                  preferred_element_type=F32).astype(BF16)
    f_t = lax.dot_general(wft_ref[...], hn, _CONTRACT_LAST, preferred_element_type=F32)
    log_f = _log_sigmoid(f_t + bf_ref[...])
    n = log_f.shape[1]
    upper = (lax.broadcasted_iota(jnp.int32, (n, n), 0)
             <= lax.broadcasted_iota(jnp.int32, (n, n), 1)).astype(F32)
    c = jnp.dot(log_f, upper, precision=HIGHEST, preferred_element_type=F32) + carry_ref[:, 0:1]
    carry_ref[...] = jnp.broadcast_to(c[:, n - 1:n], carry_ref.shape)
    c2 = c * LOG2E
    c_ref[...] = c2
    hi, mid, lo = _split3(-c2)
    parts = jnp.concatenate([hi, mid, lo, jnp.ones_like(hi)], axis=0)
    bias_lanes = lax.dot_general(parts, place_ref[...], _CONTRACT_FIRST,
                                 preferred_element_type=F32)
    k_ref[...] = (_dot(hn, wk_ref[...]) + bias_lanes).astype(BF16)


def _kv_placement(heads):
    place = jnp.zeros((4 * heads, heads * LANES), F32)
    h = jnp.arange(heads)
    for part in range(AUG_PARTS):
        place = place.at[part * heads + h, h * LANES + AUG_KSIDE_C + part].set(1.0)
        place = place.at[AUG_PARTS * heads, h * LANES + AUG_QSIDE_C + part].set(1.0)
    return place.astype(BF16)


def _shared_kv(x3d, g, w_kv, b_f):
    batch, seq, d = x3d.shape
    heads = b_f.shape[0]
    hd = d // heads
    wk = jnp.pad(w_kv[:, :d].reshape(d, heads, hd), ((0, 0), (0, 0), (0, LANES - hd)))
    wk = wk.reshape(d, heads * LANES).astype(BF16)
    wvt = w_kv[:, d:2 * d].T.astype(BF16)
    wft = w_kv[:, 2 * d:].T.astype(BF16)
    kw = heads * LANES
    return pl.pallas_call(
        _kv_kernel,
        grid=(batch, seq // KV_TILE),
        in_specs=[pl.BlockSpec((None, KV_TILE, d), lambda b, t: (b, t, 0)),
                  _resident((1, d)), _resident((d, kw)), _resident((d, d)),
                  _resident((heads, d)), _resident((heads, 1)), _resident((4 * heads, kw))],
        out_specs=[pl.BlockSpec((None, KV_TILE, kw), lambda b, t: (b, t, 0)),
                   pl.BlockSpec((None, d, KV_TILE), lambda b, t: (b, 0, t)),
                   pl.BlockSpec((None, heads, KV_TILE), lambda b, t: (b, 0, t))],
        out_shape=[jax.ShapeDtypeStruct((batch, seq, kw), BF16),
                   jax.ShapeDtypeStruct((batch, d, seq), BF16),
                   jax.ShapeDtypeStruct((batch, heads, seq), F32)],
        scratch_shapes=[pltpu.VMEM((heads, LANES), F32)],
        compiler_params=pltpu.CompilerParams(dimension_semantics=("parallel", "arbitrary"),
                                             vmem_limit_bytes=40 * MIB),
        name="shared_kv",
    )(x3d, g.reshape(1, d), wk, wvt, wft, b_f.reshape(heads, 1), _kv_placement(heads))


def _q_kernel(x_ref, g_ref, wqt_ref, c_ref, qt_ref):
    hd = FOX_HEAD_DIM
    hn = _rmsnorm(x_ref[...], g_ref[...]).astype(BF16)
    qt = lax.dot_general(wqt_ref[...], hn, _CONTRACT_LAST, preferred_element_type=F32)
    qt = (qt * (hd ** -0.5 * LOG2E)).astype(BF16)
    hi, mid, lo = (p.astype(F32) for p in _split3(c_ref[...]))
    n = hn.shape[0]
    r = lax.broadcasted_iota(jnp.int32, (BF16_SUBLANES, n), 0)
    for h in range(c_ref.shape[0]):
        base = h * LANES
        bias = jnp.where(r == 0, hi[h:h + 1], jnp.where(r == 1, mid[h:h + 1], jnp.where(
            r == 2, lo[h:h + 1], jnp.where(r < 2 * AUG_PARTS, 1.0, 0.0))))
        qt_ref[base:base + hd, :] = qt[h * hd:(h + 1) * hd, :]
        qt_ref[base + hd:base + hd + BF16_SUBLANES, :] = bias.astype(BF16)
        qt_ref[base + hd + BF16_SUBLANES:base + LANES, :] = jnp.zeros(
            (LANES - hd - BF16_SUBLANES, n), BF16)


def _q_proj(x3d, g, w_q, c2):
    batch, seq, d = x3d.shape
    heads = c2.shape[1]
    return pl.pallas_call(
        _q_kernel,
        grid=(batch, seq // ROW_TILE),
        in_specs=[pl.BlockSpec((None, ROW_TILE, d), lambda b, t: (b, t, 0)),
                  _resident((1, d)), _resident((d, d)),
                  pl.BlockSpec((None, heads, ROW_TILE), lambda b, t: (b, 0, t))],
        out_specs=pl.BlockSpec((None, heads * LANES, ROW_TILE), lambda b, t: (b, 0, t)),
        out_shape=jax.ShapeDtypeStruct((batch, heads * LANES, seq), BF16),
        compiler_params=pltpu.CompilerParams(dimension_semantics=("parallel", "parallel"),
                                             vmem_limit_bytes=40 * MIB),
        name="fox_q_proj",
    )(x3d, g.reshape(1, d), w_q.T.astype(BF16), c2)


def _fox_kernel(qt_ref, k_ref, vt_ref, o_ref, s_ref, m_ref, l_ref, acc_ref):
    t = FOX_TILE
    hd = FOX_HEAD_DIM
    i = pl.program_id(2)
    causal = (lax.broadcasted_iota(jnp.int32, (t, t), 0)
              <= lax.broadcasted_iota(jnp.int32, (t, t), 1))

    def logits(j, slot):
        keys = pl.ds(pl.multiple_of(j * t, t), t)
        for h in range(2):
            s_ref[slot, h] = _dot(k_ref[keys, h * LANES:(h + 1) * LANES],
                                  qt_ref[h * LANES:(h + 1) * LANES, :])

    def update(j, slot, masked):
        keys = pl.ds(pl.multiple_of(j * t, t), t)
        for h in range(2):
            def tile():
                s = s_ref[slot, h]
                return jnp.where(causal, s, NEG_BIG) if masked else s
            m = m_ref[h]
            m_new = jnp.maximum(m, jnp.max(tile(), axis=0, keepdims=True))
            alpha = jnp.exp2(m - m_new)
            p = jnp.exp2(tile() - m_new)
            m_ref[h] = m_new
            l_ref[h] = alpha * l_ref[h] + jnp.sum(p, axis=0, keepdims=True)
            acc_ref[h] = alpha * acc_ref[h] + _dot(vt_ref[h * hd:(h + 1) * hd, keys], p.astype(BF16))

    m_ref[...] = jnp.full(m_ref.shape, NEG_BIG, F32)
    l_ref[...] = jnp.zeros(l_ref.shape, F32)
    acc_ref[...] = jnp.zeros(acc_ref.shape, F32)
    logits(0, 0)

    def pair(jj, carry):
        j = 2 * jj
        logits(j + 1, 1)
        update(j, 0, False)
        logits(j + 2, 0)
        update(j + 1, 1, False)
        return carry

    lax.fori_loop(0, i // 2, pair, 0)

    @pl.when(i % 2 == 1)
    def _():
        logits(i, 1)
        update(i - 1, 0, False)

    update(i, i % 2, True)
    o_t = jnp.concatenate([acc_ref[0] / l_ref[0], acc_ref[1] / l_ref[1]], axis=0)
    o_ref[...] = o_t.T.astype(BF16)


def _fox_attention(qt, k_aug, vt, batch, seq, d):
    pairs = d // (2 * FOX_HEAD_DIM)
    t = FOX_TILE
    out = pl.pallas_call(
        _fox_kernel,
        grid=(batch, pairs, seq // t),
        in_specs=[pl.BlockSpec((None, 2 * LANES, t), lambda b, p, i: (b, p, i)),
                  pl.BlockSpec((None, seq, 2 * LANES), lambda b, p, i: (b, 0, p)),
                  pl.BlockSpec((None, 2 * FOX_HEAD_DIM, seq), lambda b, p, i: (b, p, 0))],
        out_specs=pl.BlockSpec((None, t, 2 * FOX_HEAD_DIM), lambda b, p, i: (b, i, p)),
        out_shape=jax.ShapeDtypeStruct((batch, seq, d), BF16),
        scratch_shapes=[pltpu.VMEM((2, 2, t, t), F32),
                        pltpu.VMEM((2, 1, t), F32), pltpu.VMEM((2, 1, t), F32),
                        pltpu.VMEM((2, FOX_HEAD_DIM, t), F32)],
        compiler_params=pltpu.CompilerParams(
            dimension_semantics=("parallel", "parallel", "arbitrary"),
            vmem_limit_bytes=40 * MIB),
        name="fox_attention",
    )(qt, k_aug, vt)
    return out.reshape(batch * seq, d)


def kernel(x, attn_norm, ffn_norm, gla_w_in, gla_w_alpha_up, gla_b_alpha, gla_g_head, gla_w_out,
           kv_norm, w_kv, b_f, fox_w_q, fox_w_o, ffn_w_gu, ffn_w_down, final_norm):
    batch, seq, d = x.shape
    depth = attn_norm.shape[0]
    n_gla = gla_w_in.shape[0]
    assert (batch * seq) % ROW_TILE == 0 and seq % GLA_TIME_TILE == 0 and seq % ROW_TILE == 0
    assert seq % KV_TILE == 0 and seq % FOX_TILE == 0 and 2 * FOX_HEAD_DIM == LANES
    assert d // b_f.shape[0] == FOX_HEAD_DIM

    h = x.reshape(batch * seq, d)
    k_aug = v_t = c2 = None
    for layer in range(depth):
        if layer < n_gla:
            q, k, v, r, la = _gla_in_proj(h, attn_norm[layer], gla_w_in[layer],
                                          gla_w_alpha_up[layer], gla_b_alpha[layer])
            o = _gla_recurrence(q, k, v, r, la, gla_g_head[layer], batch, seq)
            w_o = gla_w_out[layer]
        else:
            h3 = h.reshape(batch, seq, d)
            if layer == n_gla:
                k_aug, v_t, c2 = _shared_kv(h3, kv_norm, w_kv, b_f)
            j = layer - n_gla
            q_t = _q_proj(h3, attn_norm[layer], fox_w_q[j], c2)
            o = _fox_attention(q_t, k_aug, v_t, batch, seq, d)
            w_o = fox_w_o[j]
        h = _out_ffn(h, o, w_o, ffn_norm[layer], ffn_w_gu[layer], ffn_w_down[layer],
                     final_norm, final=(layer == depth - 1))
    return h.reshape(batch, seq, d)
```

```python
import functools
import math

import jax
import jax.numpy as jnp
from jax import lax
from jax.experimental import pallas as pl
from jax.experimental.pallas import tpu as pltpu

F32 = jnp.float32
BF16 = jnp.bfloat16
HIGHEST = lax.Precision.HIGHEST

EPS = 1e-6
GLA_HEADS = 4
GLA_RANK = 16
GLA_TAU = 16.0
GLA_CHUNK = 64
FOX_HEAD_DIM = 64
LOG2E = math.log2(math.e)

LANES = 128
BF16_SUBLANES = 16
MIB = 1024 * 1024

ROW_TILE = 512
GLA_TIME_TILE = 512
KV_TILE = 512
FOX_TILE = 256
FOX_HEADS_PER_STEP = 2
FOX_SLOTS = 4
FOX_LOOKAHEAD = 2
FFN_CHUNKS = 2
NEG_BIG = -1e30

_CONTRACT_LAST = (((1,), (1,)), ((), ()))
_CONTRACT_FIRST = (((0,), (0,)), ((), ()))


def _rmsnorm(x, g):
    return x * lax.rsqrt(jnp.mean(x * x, axis=-1, keepdims=True) + EPS) * g


def _log_sigmoid(z):
    return jnp.minimum(z, 0.0) - jnp.log(1.0 + jnp.exp(-jnp.abs(z)))


def _silu(x):
    return x * jax.nn.sigmoid(x)


def _dot(a, b):
    return jnp.dot(a, b, preferred_element_type=F32)


def _resident(shape):
    return pl.BlockSpec(shape, lambda *_: (0,) * len(shape), pipeline_mode=pl.Buffered(1))


def _gla_in_kernel(x_ref, g_ref, wq_ref, wk_ref, wv_ref, wr_ref, wa_ref, wup_ref, ba_ref,
                   q_ref, k_ref, v_ref, r_ref, la_ref):
    hn = _rmsnorm(x_ref[...], g_ref[...]).astype(BF16)
    q_ref[...] = _dot(hn, wq_ref[...]).astype(BF16)
    k_ref[...] = _dot(hn, wk_ref[...]).astype(BF16)
    v_ref[...] = _dot(hn, wv_ref[...]).astype(BF16)
    r_ref[...] = _dot(hn, wr_ref[...]).astype(BF16)
    a_low = _dot(hn, wa_ref[...])
    z = jnp.dot(a_low, wup_ref[...], precision=HIGHEST, preferred_element_type=F32) + ba_ref[...]
    la_ref[...] = _log_sigmoid(z) * (1.0 / GLA_TAU)


def _gla_in_proj(x2d, g, w_in, w_up, b_alpha):
    m, d = x2d.shape
    qk = w_up.shape[1]
    vdim = (w_in.shape[1] - 2 * qk - GLA_RANK) // 2
    wq = w_in[:, :qk].astype(BF16)
    wk = w_in[:, qk:2 * qk].astype(BF16)
    wv = w_in[:, 2 * qk:2 * qk + vdim].astype(BF16)
    wr = w_in[:, 2 * qk + vdim:2 * qk + 2 * vdim].astype(BF16)
    wa = jnp.pad(w_in[:, 2 * qk + 2 * vdim:], ((0, 0), (0, LANES - GLA_RANK))).astype(BF16)
    wup = jnp.pad(w_up, ((0, LANES - GLA_RANK), (0, 0)))
    row = lambda n: pl.BlockSpec((ROW_TILE, n), lambda i: (i, 0))
    return pl.pallas_call(
        _gla_in_kernel,
        grid=(m // ROW_TILE,),
        in_specs=[row(d), _resident((1, d)), _resident((d, qk)), _resident((d, qk)),
                  _resident((d, vdim)), _resident((d, vdim)), _resident((d, LANES)),
                  _resident((LANES, qk)), _resident((1, qk))],
        out_specs=[row(qk), row(qk), row(vdim), row(vdim), row(qk)],
        out_shape=[jax.ShapeDtypeStruct((m, qk), BF16), jax.ShapeDtypeStruct((m, qk), BF16),
                   jax.ShapeDtypeStruct((m, vdim), BF16), jax.ShapeDtypeStruct((m, vdim), BF16),
                   jax.ShapeDtypeStruct((m, qk), F32)],
        compiler_params=pltpu.CompilerParams(dimension_semantics=("parallel",),
                                             vmem_limit_bytes=40 * MIB),
        name="gla_in_proj",
    )(x2d, g.reshape(1, d), wq, wk, wv, wr, wa, wup, b_alpha.reshape(1, qk))


def _gla_kernel(q_ref, k_ref, v_ref, r_ref, la_ref, g_ref, o_ref, st_ref, *, dk, dv, n_chunks):
    c = GLA_CHUNK

    @pl.when(pl.program_id(1) == 0)
    def _():
        st_ref[...] = jnp.zeros_like(st_ref)

    causal = (lax.broadcasted_iota(jnp.int32, (c, c), 0)
              >= lax.broadcasted_iota(jnp.int32, (c, c), 1))
    qk = GLA_HEADS * dk
    row = lax.broadcasted_iota(jnp.int32, (c, qk), 0)
    g = g_ref[...]
    scale = dk ** -0.5
    heads = range(GLA_HEADS)
    kcols = [slice(h * dk, (h + 1) * dk) for h in heads]
    vcols = [slice(h * dv, (h + 1) * dv) for h in heads]

    for n in range(n_chunks):
        rows = pl.ds(n * c, c)
        b = la_ref[rows, :]
        shift = 1
        while shift < c:
            b = b + jnp.where(row >= shift, pltpu.roll(b, shift, axis=0), 0.0)
            shift *= 2
        b_last = b[c - 1:c, :]
        qf = q_ref[rows, :].astype(F32) * scale
        kf = k_ref[rows, :].astype(F32)
        q_dec = (qf * jnp.exp(b)).astype(BF16)
        k_dec = (kf * jnp.exp(-b)).astype(BF16)
        k_out = (kf * jnp.exp(b_last - b)).astype(BF16)
        decay = jnp.exp(b_last)
        v = [v_ref[rows, vcols[h]] for h in heads]
        a = [lax.dot_general(q_dec[:, kcols[h]], k_dec[:, kcols[h]], _CONTRACT_LAST,
                             preferred_element_type=F32) for h in heads]
        a = [jnp.where(causal, a[h], 0.0).astype(BF16) for h in heads]
        st = [st_ref[h] for h in heads]
        o = [_dot(a[h], v[h]) + lax.dot_general(q_dec[:, kcols[h]], st[h].astype(BF16),
                                                _CONTRACT_LAST, preferred_element_type=F32)
             for h in heads]
        kv = [lax.dot_general(v[h], k_out[:, kcols[h]], _CONTRACT_FIRST,
                              preferred_element_type=F32) for h in heads]
        for h in heads:
            st_ref[h] = st[h] * decay[:, kcols[h]] + kv[h]
            gate = _silu(r_ref[rows, vcols[h]].astype(F32))
            o_ref[rows, vcols[h]] = (_rmsnorm(o[h], g) * gate).astype(BF16)


def _gla_recurrence(q, k, v, r, la, g_head, batch, seq):
    qk = q.shape[-1]
    vdim = v.shape[-1]
    dk, dv = qk // GLA_HEADS, vdim // GLA_HEADS
    tt = GLA_TIME_TILE
    shp = lambda a: a.reshape(batch, seq, a.shape[-1])
    spec = lambda n: pl.BlockSpec((None, tt, n), lambda b, t: (b, t, 0))
    out = pl.pallas_call(
        functools.partial(_gla_kernel, dk=dk, dv=dv, n_chunks=tt // GLA_CHUNK),
        grid=(batch, seq // tt),
        in_specs=[spec(qk), spec(qk), spec(vdim), spec(vdim), spec(qk), _resident((1, dv))],
        out_specs=spec(vdim),
        out_shape=jax.ShapeDtypeStruct((batch, seq, vdim), BF16),
        scratch_shapes=[pltpu.VMEM((GLA_HEADS, dv, dk), F32)],
        compiler_params=pltpu.CompilerParams(dimension_semantics=("parallel", "arbitrary"),
                                             vmem_limit_bytes=40 * MIB),
        name="gla_recurrence",
    )(shp(q), shp(k), shp(v), shp(r), shp(la), g_head.reshape(1, dv))
    return out.reshape(batch * seq, vdim)


def _out_ffn_kernel(x_ref, o_ref, wo_ref, g_ref, wg_ref, wu_ref, wd_ref, gf_ref, y_ref, *, final):
    x1 = x_ref[...] + _dot(o_ref[...], wo_ref[...])
    hn = _rmsnorm(x1, g_ref[...]).astype(BF16)
    dff = wg_ref.shape[1]
    fc = dff // FFN_CHUNKS
    acc = x1
    for ci in range(FFN_CHUNKS):
        cols = slice(ci * fc, (ci + 1) * fc)
        gate = _dot(hn, wg_ref[:, cols])
        up = _dot(hn, wu_ref[:, cols])
        acc = acc + _dot((_silu(gate) * up).astype(BF16), wd_ref[cols, :])
    if final:
        acc = _rmsnorm(acc, gf_ref[...])
    y_ref[...] = acc


def _out_ffn(x2d, o2d, w_o, g_ffn, w_gu, w_down, g_final, final):
    m, d = x2d.shape
    dff = w_down.shape[0]
    assert dff % (FFN_CHUNKS * LANES) == 0
    wg = w_gu[:, :dff].astype(BF16)
    wu = w_gu[:, dff:].astype(BF16)
    row = lambda: pl.BlockSpec((ROW_TILE, d), lambda i: (i, 0))
    return pl.pallas_call(
        functools.partial(_out_ffn_kernel, final=final),
        grid=(m // ROW_TILE,),
        in_specs=[row(), row(), _resident((d, d)), _resident((1, d)), _resident((d, dff)),
                  _resident((d, dff)), _resident((dff, d)), _resident((1, d))],
        out_specs=row(),
        out_shape=jax.ShapeDtypeStruct((m, d), F32),
        compiler_params=pltpu.CompilerParams(dimension_semantics=("parallel",),
                                             vmem_limit_bytes=52 * MIB),
        name="out_proj_ffn",
    )(x2d, o2d, w_o.astype(BF16), g_ffn.reshape(1, d), wg, wu, w_down.astype(BF16),
      g_final.reshape(1, d))


AUG_QSIDE_C = FOX_HEAD_DIM
AUG_KSIDE_C = FOX_HEAD_DIM + 3
AUG_PARTS = 3


def _split3(x):
    hi = x.astype(BF16)
    r1 = x - hi.astype(F32)
    mid = r1.astype(BF16)
    lo = (r1 - mid.astype(F32)).astype(BF16)
    return hi, mid, lo


def _kv_kernel(x_ref, g_ref, wk_ref, wvt_ref, wft_ref, bf_ref, place_ref,
               k_ref, vt_ref, c_ref, carry_ref):
    @pl.when(pl.program_id(1) == 0)
    def _():
        carry_ref[...] = jnp.zeros_like(carry_ref)

    hn = _rmsnorm(x_ref[...], g_ref[...]).astype(BF16)
    vt_ref[...] = lax.dot_general(wvt_ref[...], hn, _CONTRACT_LAST,
                                  preferred_element_type=F32).astype(BF16)
    f_t = lax.dot_general(wft_ref[...], hn, _CONTRACT_LAST, preferred_element_type=F32)
    log_f = _log_sigmoid(f_t + bf_ref[...])
    n = log_f.shape[1]
    upper = (lax.broadcasted_iota(jnp.int32, (n, n), 0)
             <= lax.broadcasted_iota(jnp.int32, (n, n), 1)).astype(F32)
    c = jnp.dot(log_f, upper, precision=HIGHEST, preferred_element_type=F32) + carry_ref[:, 0:1]
    carry_ref[...] = jnp.broadcast_to(c[:, n - 1:n], carry_ref.shape)
    c2 = c * LOG2E
    c_ref[...] = c2
    hi, mid, lo = _split3(-c2)
    parts = jnp.concatenate([hi, mid, lo, jnp.ones_like(hi)], axis=0)
    bias_lanes = lax.dot_general(parts, place_ref[...], _CONTRACT_FIRST,
                                 preferred_element_type=F32)
    k_ref[...] = (_dot(hn, wk_ref[...]) + bias_lanes).astype(BF16)


def _kv_placement(heads):
    place = jnp.zeros((4 * heads, heads * LANES), F32)
    h = jnp.arange(heads)
    for part in range(AUG_PARTS):
        place = place.at[part * heads + h, h * LANES + AUG_KSIDE_C + part].set(1.0)
        place = place.at[AUG_PARTS * heads, h * LANES + AUG_QSIDE_C + part].set(1.0)
    return place.astype(BF16)


def _shared_kv(x3d, g, w_kv, b_f):
    batch, seq, d = x3d.shape
    heads = b_f.shape[0]
    hd = d // heads
    wk = jnp.pad(w_kv[:, :d].reshape(d, heads, hd), ((0, 0), (0, 0), (0, LANES - hd)))
    wk = wk.reshape(d, heads * LANES).astype(BF16)
    wvt = w_kv[:, d:2 * d].T.astype(BF16)
    wft = w_kv[:, 2 * d:].T.astype(BF16)
    kw = heads * LANES
    return pl.pallas_call(
        _kv_kernel,
        grid=(batch, seq // KV_TILE),
        in_specs=[pl.BlockSpec((None, KV_TILE, d), lambda b, t: (b, t, 0)),
                  _resident((1, d)), _resident((d, kw)), _resident((d, d)),
                  _resident((heads, d)), _resident((heads, 1)), _resident((4 * heads, kw))],
        out_specs=[pl.BlockSpec((None, KV_TILE, kw), lambda b, t: (b, t, 0)),
                   pl.BlockSpec((None, d, KV_TILE), lambda b, t: (b, 0, t)),
                   pl.BlockSpec((None, heads, KV_TILE), lambda b, t: (b, 0, t))],
        out_shape=[jax.ShapeDtypeStruct((batch, seq, kw), BF16),
                   jax.ShapeDtypeStruct((batch, d, seq), BF16),
                   jax.ShapeDtypeStruct((batch, heads, seq), F32)],
        scratch_shapes=[pltpu.VMEM((heads, LANES), F32)],
        compiler_params=pltpu.CompilerParams(dimension_semantics=("parallel", "arbitrary"),
                                             vmem_limit_bytes=40 * MIB),
        name="shared_kv",
    )(x3d, g.reshape(1, d), wk, wvt, wft, b_f.reshape(heads, 1), _kv_placement(heads))


def _q_kernel(x_ref, g_ref, wqt_ref, c_ref, qt_ref):
    hd = FOX_HEAD_DIM
    hn = _rmsnorm(x_ref[...], g_ref[...]).astype(BF16)
    qt = lax.dot_general(wqt_ref[...], hn, _CONTRACT_LAST, preferred_element_type=F32)
    qt = (qt * (hd ** -0.5 * LOG2E)).astype(BF16)
    hi, mid, lo = (p.astype(F32) for p in _split3(c_ref[...]))
    n = hn.shape[0]
    r = lax.broadcasted_iota(jnp.int32, (BF16_SUBLANES, n), 0)
    for h in range(c_ref.shape[0]):
        base = h * LANES
        bias = jnp.where(r == 0, hi[h:h + 1], jnp.where(r == 1, mid[h:h + 1], jnp.where(
            r == 2, lo[h:h + 1], jnp.where(r < 2 * AUG_PARTS, 1.0, 0.0))))
        qt_ref[base:base + hd, :] = qt[h * hd:(h + 1) * hd, :]
        qt_ref[base + hd:base + hd + BF16_SUBLANES, :] = bias.astype(BF16)
        qt_ref[base + hd + BF16_SUBLANES:base + LANES, :] = jnp.zeros(
            (LANES - hd - BF16_SUBLANES, n), BF16)


def _q_proj(x3d, g, w_q, c2):
    batch, seq, d = x3d.shape
    heads = c2.shape[1]
    return pl.pallas_call(
        _q_kernel,
        grid=(batch, seq // ROW_TILE),
        in_specs=[pl.BlockSpec((None, ROW_TILE, d), lambda b, t: (b, t, 0)),
                  _resident((1, d)), _resident((d, d)),
                  pl.BlockSpec((None, heads, ROW_TILE), lambda b, t: (b, 0, t))],
        out_specs=pl.BlockSpec((None, heads * LANES, ROW_TILE), lambda b, t: (b, 0, t)),
        out_shape=jax.ShapeDtypeStruct((batch, heads * LANES, seq), BF16),
        compiler_params=pltpu.CompilerParams(dimension_semantics=("parallel", "parallel"),
                                             vmem_limit_bytes=40 * MIB),
        name="fox_q_proj",
    )(x3d, g.reshape(1, d), w_q.T.astype(BF16), c2)


def _fox_kernel(qt_ref, k_ref, vt_ref, o_ref, s_ref, m_ref, acc_ref, *, n_q):
    t = FOX_TILE
    hd = FOX_HEAD_DIM
    heads = range(FOX_HEADS_PER_STEP)
    causal = (lax.broadcasted_iota(jnp.int32, (t, t), 0)
              <= lax.broadcasted_iota(jnp.int32, (t, t), 1))
    ones = jnp.ones((BF16_SUBLANES, t), BF16)

    def logits(qi, j, slot):
        keys = pl.ds(j * t, t)
        queries = pl.ds(qi * t, t)
        for h in heads:
            s_ref[slot, h] = _dot(k_ref[keys, h * LANES:(h + 1) * LANES],
                                  qt_ref[h * LANES:(h + 1) * LANES, queries])

    def update(qi, j, slot, diagonal):
        keys = pl.ds(j * t, t)

        def tile(h):
            s = s_ref[slot, h]
            return jnp.where(causal, s, NEG_BIG) if diagonal else s

        if diagonal:
            m_new = [jnp.max(tile(h), axis=0, keepdims=True) for h in heads]
        else:
            m_old = [m_ref[qi, h] for h in heads]
            m_new = [jnp.maximum(m_old[h], jnp.max(tile(h), axis=0, keepdims=True)) for h in heads]
            alpha = [jnp.exp2(m_old[h] - m_new[h]) for h in heads]
        p = [jnp.exp2(tile(h) - m_new[h]).astype(BF16) for h in heads]
        pv = [_dot(jnp.concatenate([vt_ref[h * hd:(h + 1) * hd, keys], ones], axis=0), p[h])
              for h in heads]
        for h in heads:
            m_ref[qi, h] = m_new[h]
            acc_ref[qi, h] = pv[h] if diagonal else alpha[h] * acc_ref[qi, h] + pv[h]

    ns, ahead = FOX_SLOTS, FOX_LOOKAHEAD
    tiles = [(qi, qi) for qi in range(n_q)]
    tiles += [(qi, j) for j in range(n_q) for qi in range(j + 1, n_q)]
    for w in range(ahead):
        logits(*tiles[w], w)
    for k, (qi, j) in enumerate(tiles):
        if k + ahead < len(tiles):
            logits(*tiles[k + ahead], (k + ahead) % ns)
        update(qi, j, k % ns, qi == j)

    for qi in range(n_q):
        o_t = jnp.concatenate([acc_ref[qi, h, :hd] / acc_ref[qi, h, hd:hd + 1] for h in heads],
                              axis=0)
        o_ref[pl.ds(qi * t, t), :] = o_t.T.astype(BF16)


def _fox_attention(qt, k_aug, vt, batch, seq, d):
    nh = FOX_HEADS_PER_STEP
    groups = d // (nh * FOX_HEAD_DIM)
    t = FOX_TILE
    out = pl.pallas_call(
        functools.partial(_fox_kernel, n_q=seq // t),
        grid=(batch, groups),
        in_specs=[pl.BlockSpec((None, nh * LANES, seq), lambda b, p: (b, p, 0)),
                  pl.BlockSpec((None, seq, nh * LANES), lambda b, p: (b, 0, p)),
                  pl.BlockSpec((None, nh * FOX_HEAD_DIM, seq), lambda b, p: (b, p, 0))],
        out_specs=pl.BlockSpec((None, seq, nh * FOX_HEAD_DIM), lambda b, p: (b, 0, p)),
        out_shape=jax.ShapeDtypeStruct((batch, seq, d), BF16),
        scratch_shapes=[pltpu.VMEM((FOX_SLOTS, nh, t, t), F32),
                        pltpu.VMEM((seq // t, nh, 1, t), F32),
                        pltpu.VMEM((seq // t, nh, FOX_HEAD_DIM + BF16_SUBLANES, t), F32)],
        compiler_params=pltpu.CompilerParams(
            dimension_semantics=("parallel", "parallel"),
            vmem_limit_bytes=40 * MIB),
        name="fox_attention",
    )(qt, k_aug, vt)
    return out.reshape(batch * seq, d)


def kernel(x, attn_norm, ffn_norm, gla_w_in, gla_w_alpha_up, gla_b_alpha, gla_g_head, gla_w_out,
           kv_norm, w_kv, b_f, fox_w_q, fox_w_o, ffn_w_gu, ffn_w_down, final_norm):
    batch, seq, d = x.shape
    depth = attn_norm.shape[0]
    n_gla = gla_w_in.shape[0]
    assert (batch * seq) % ROW_TILE == 0 and seq % GLA_TIME_TILE == 0 and seq % ROW_TILE == 0
    assert seq % KV_TILE == 0 and seq % FOX_TILE == 0
    assert (FOX_HEADS_PER_STEP * FOX_HEAD_DIM) % LANES == 0
    assert d // b_f.shape[0] == FOX_HEAD_DIM

    h = x.reshape(batch * seq, d)
    k_aug = v_t = c2 = None
    for layer in range(depth):
        if layer < n_gla:
            q, k, v, r, la = _gla_in_proj(h, attn_norm[layer], gla_w_in[layer],
                                          gla_w_alpha_up[layer], gla_b_alpha[layer])
            o = _gla_recurrence(q, k, v, r, la, gla_g_head[layer], batch, seq)
            w_o = gla_w_out[layer]
        else:
            h3 = h.reshape(batch, seq, d)
            if layer == n_gla:
                k_aug, v_t, c2 = _shared_kv(h3, kv_norm, w_kv, b_f)
            j = layer - n_gla
            q_t = _q_proj(h3, attn_norm[layer], fox_w_q[j], c2)
            o = _fox_attention(q_t, k_aug, v_t, batch, seq, d)
            w_o = fox_w_o[j]
        h = _out_ffn(h, o, w_o, ffn_norm[layer], ffn_w_gu[layer], ffn_w_down[layer],
                     final_norm, final=(layer == depth - 1))
    return h.reshape(batch, seq, d)
```

```python
import functools
import math

import jax
import jax.numpy as jnp
from jax import lax
from jax.experimental import pallas as pl
from jax.experimental.pallas import tpu as pltpu

F32 = jnp.float32
BF16 = jnp.bfloat16
HIGHEST = lax.Precision.HIGHEST

EPS = 1e-6
GLA_HEADS = 4
GLA_RANK = 16
GLA_TAU = 16.0
GLA_CHUNK = 64
FOX_HEAD_DIM = 64
LOG2E = math.log2(math.e)

LANES = 128
BF16_SUBLANES = 16
MIB = 1024 * 1024

ROW_TILE = 512
GLA_TIME_TILE = 512
KV_TILE = 512
FOX_TILE = 256
FOX_HEADS_PER_STEP = 2
FOX_SLOTS = 4
FOX_LOOKAHEAD = 2
FFN_CHUNKS = 1
NEG_BIG = -1e30

_CONTRACT_LAST = (((1,), (1,)), ((), ()))
_CONTRACT_FIRST = (((0,), (0,)), ((), ()))


def _rmsnorm(x, g):
    return x * lax.rsqrt(jnp.mean(x * x, axis=-1, keepdims=True) + EPS) * g


def _log_sigmoid(z):
    return jnp.minimum(z, 0.0) - jnp.log(1.0 + jnp.exp(-jnp.abs(z)))


def _silu(x):
    return x * jax.nn.sigmoid(x)


def _dot(a, b):
    return jnp.dot(a, b, preferred_element_type=F32)


def _resident(shape):
    return pl.BlockSpec(shape, lambda *_: (0,) * len(shape), pipeline_mode=pl.Buffered(1))


def _gla_in_kernel(x_ref, g_ref, wq_ref, wk_ref, wv_ref, wr_ref, wa_ref, wup_ref, ba_ref,
                   q_ref, k_ref, v_ref, r_ref, la_ref):
    hn = _rmsnorm(x_ref[...], g_ref[...]).astype(BF16)
    q_ref[...] = _dot(hn, wq_ref[...]).astype(BF16)
    k_ref[...] = _dot(hn, wk_ref[...]).astype(BF16)
    v_ref[...] = _dot(hn, wv_ref[...]).astype(BF16)
    r_ref[...] = _dot(hn, wr_ref[...]).astype(BF16)
    a_low = _dot(hn, wa_ref[...])
    z = _dot(a_low.astype(BF16), wup_ref[...]) + ba_ref[...]
    la_ref[...] = _log_sigmoid(z) * (1.0 / GLA_TAU)


def _gla_in_proj(x2d, g, w_in, w_up, b_alpha):
    m, d = x2d.shape
    qk = w_up.shape[1]
    vdim = (w_in.shape[1] - 2 * qk - GLA_RANK) // 2
    wq = w_in[:, :qk].astype(BF16)
    wk = w_in[:, qk:2 * qk].astype(BF16)
    wv = w_in[:, 2 * qk:2 * qk + vdim].astype(BF16)
    wr = w_in[:, 2 * qk + vdim:2 * qk + 2 * vdim].astype(BF16)
    wa = jnp.pad(w_in[:, 2 * qk + 2 * vdim:], ((0, 0), (0, LANES - GLA_RANK))).astype(BF16)
    wup = jnp.pad(w_up, ((0, LANES - GLA_RANK), (0, 0))).astype(BF16)
    row = lambda n: pl.BlockSpec((ROW_TILE, n), lambda i: (i, 0))
    return pl.pallas_call(
        _gla_in_kernel,
        grid=(m // ROW_TILE,),
        in_specs=[row(d), _resident((1, d)), _resident((d, qk)), _resident((d, qk)),
                  _resident((d, vdim)), _resident((d, vdim)), _resident((d, LANES)),
                  _resident((LANES, qk)), _resident((1, qk))],
        out_specs=[row(qk), row(qk), row(vdim), row(vdim), row(qk)],
        out_shape=[jax.ShapeDtypeStruct((m, qk), BF16), jax.ShapeDtypeStruct((m, qk), BF16),
                   jax.ShapeDtypeStruct((m, vdim), BF16), jax.ShapeDtypeStruct((m, vdim), BF16),
                   jax.ShapeDtypeStruct((m, qk), F32)],
        compiler_params=pltpu.CompilerParams(dimension_semantics=("parallel",),
                                             vmem_limit_bytes=40 * MIB),
        name="gla_in_proj",
    )(x2d, g.reshape(1, d), wq, wk, wv, wr, wa, wup, b_alpha.reshape(1, qk))


def _gla_kernel(q_ref, k_ref, v_ref, r_ref, la_ref, g_ref, o_ref, st_ref, *, dk, dv, n_chunks):
    c = GLA_CHUNK

    @pl.when(pl.program_id(1) == 0)
    def _():
        st_ref[...] = jnp.zeros_like(st_ref)

    causal = (lax.broadcasted_iota(jnp.int32, (c, c), 0)
              >= lax.broadcasted_iota(jnp.int32, (c, c), 1))
    qk = GLA_HEADS * dk
    row = lax.broadcasted_iota(jnp.int32, (c, qk), 0)
    g = g_ref[...]
    scale = dk ** -0.5
    heads = range(GLA_HEADS)
    kcols = [slice(h * dk, (h + 1) * dk) for h in heads]
    vcols = [slice(h * dv, (h + 1) * dv) for h in heads]

    for n in range(n_chunks):
        rows = pl.ds(n * c, c)
        b = la_ref[rows, :]
        shift = 1
        while shift < c:
            b = b + jnp.where(row >= shift, pltpu.roll(b, shift, axis=0), 0.0)
            shift *= 2
        b_last = b[c - 1:c, :]
        qf = q_ref[rows, :].astype(F32) * scale
        kf = k_ref[rows, :].astype(F32)
        q_dec = (qf * jnp.exp(b)).astype(BF16)
        k_dec = (kf * jnp.exp(-b)).astype(BF16)
        k_out = (kf * jnp.exp(b_last - b)).astype(BF16)
        decay = jnp.exp(b_last)
        v = [v_ref[rows, vcols[h]] for h in heads]
        a = [lax.dot_general(q_dec[:, kcols[h]], k_dec[:, kcols[h]], _CONTRACT_LAST,
                             preferred_element_type=F32) for h in heads]
        a = [jnp.where(causal, a[h], 0.0).astype(BF16) for h in heads]
        st = [st_ref[h] for h in heads]
        o = [_dot(a[h], v[h]) + lax.dot_general(q_dec[:, kcols[h]], st[h].astype(BF16),
                                                _CONTRACT_LAST, preferred_element_type=F32)
             for h in heads]
        kv = [lax.dot_general(v[h], k_out[:, kcols[h]], _CONTRACT_FIRST,
                              preferred_element_type=F32) for h in heads]
        for h in heads:
            st_ref[h] = st[h] * decay[:, kcols[h]] + kv[h]
            gate = _silu(r_ref[rows, vcols[h]].astype(F32))
            o_ref[rows, vcols[h]] = (_rmsnorm(o[h], g) * gate).astype(BF16)


def _gla_recurrence(q, k, v, r, la, g_head, batch, seq):
    qk = q.shape[-1]
    vdim = v.shape[-1]
    dk, dv = qk // GLA_HEADS, vdim // GLA_HEADS
    tt = GLA_TIME_TILE
    shp = lambda a: a.reshape(batch, seq, a.shape[-1])
    spec = lambda n: pl.BlockSpec((None, tt, n), lambda b, t: (b, t, 0))
    out = pl.pallas_call(
        functools.partial(_gla_kernel, dk=dk, dv=dv, n_chunks=tt // GLA_CHUNK),
        grid=(batch, seq // tt),
        in_specs=[spec(qk), spec(qk), spec(vdim), spec(vdim), spec(qk), _resident((1, dv))],
        out_specs=spec(vdim),
        out_shape=jax.ShapeDtypeStruct((batch, seq, vdim), BF16),
        scratch_shapes=[pltpu.VMEM((GLA_HEADS, dv, dk), F32)],
        compiler_params=pltpu.CompilerParams(dimension_semantics=("parallel", "arbitrary"),
                                             vmem_limit_bytes=40 * MIB),
        name="gla_recurrence",
    )(shp(q), shp(k), shp(v), shp(r), shp(la), g_head.reshape(1, dv))
    return out.reshape(batch * seq, vdim)


def _out_ffn_kernel(x_ref, o_ref, wo_ref, g_ref, wg_ref, wu_ref, wd_ref, gf_ref, y_ref, *, final):
    x1 = x_ref[...] + _dot(o_ref[...], wo_ref[...])
    hn = _rmsnorm(x1, g_ref[...]).astype(BF16)
    dff = wg_ref.shape[1]
    fc = dff // FFN_CHUNKS
    acc = x1
    for ci in range(FFN_CHUNKS):
        cols = slice(ci * fc, (ci + 1) * fc)
        gate = _dot(hn, wg_ref[:, cols])
        up = _dot(hn, wu_ref[:, cols])
        acc = acc + _dot((_silu(gate) * up).astype(BF16), wd_ref[cols, :])
    if final:
        acc = _rmsnorm(acc, gf_ref[...])
    y_ref[...] = acc


def _out_ffn(x2d, o2d, w_o, g_ffn, w_gu, w_down, g_final, final):
    m, d = x2d.shape
    dff = w_down.shape[0]
    assert dff % (FFN_CHUNKS * LANES) == 0
    wg = w_gu[:, :dff].astype(BF16)
    wu = w_gu[:, dff:].astype(BF16)
    row = lambda: pl.BlockSpec((ROW_TILE, d), lambda i: (i, 0))
    return pl.pallas_call(
        functools.partial(_out_ffn_kernel, final=final),
        grid=(m // ROW_TILE,),
        in_specs=[row(), row(), _resident((d, d)), _resident((1, d)), _resident((d, dff)),
                  _resident((d, dff)), _resident((dff, d)), _resident((1, d))],
        out_specs=row(),
        out_shape=jax.ShapeDtypeStruct((m, d), F32),
        compiler_params=pltpu.CompilerParams(dimension_semantics=("parallel",),
                                             vmem_limit_bytes=52 * MIB),
        name="out_proj_ffn",
    )(x2d, o2d, w_o.astype(BF16), g_ffn.reshape(1, d), wg, wu, w_down.astype(BF16),
      g_final.reshape(1, d))


AUG_QSIDE_C = FOX_HEAD_DIM
AUG_KSIDE_C = FOX_HEAD_DIM + 3
AUG_PARTS = 3


def _split3(x):
    hi = x.astype(BF16)
    r1 = x - hi.astype(F32)
    mid = r1.astype(BF16)
    lo = (r1 - mid.astype(F32)).astype(BF16)
    return hi, mid, lo


def _kv_kernel(x_ref, g_ref, wk_ref, wvt_ref, wft_ref, bf_ref, place_ref,
               k_ref, vt_ref, c_ref, carry_ref):
    @pl.when(pl.program_id(1) == 0)
    def _():
        carry_ref[...] = jnp.zeros_like(carry_ref)

    hn = _rmsnorm(x_ref[...], g_ref[...]).astype(BF16)
    vt_ref[...] = lax.dot_general(wvt_ref[...], hn, _CONTRACT_LAST,
                                  preferred_element_type=F32).astype(BF16)
    f_t = lax.dot_general(wft_ref[...], hn, _CONTRACT_LAST, preferred_element_type=F32)
    log_f = _log_sigmoid(f_t + bf_ref[...])
    n = log_f.shape[1]
    upper = (lax.broadcasted_iota(jnp.int32, (n, n), 0)
             <= lax.broadcasted_iota(jnp.int32, (n, n), 1)).astype(F32)
    c = jnp.dot(log_f, upper, precision=HIGHEST, preferred_element_type=F32) + carry_ref[:, 0:1]
    carry_ref[...] = jnp.broadcast_to(c[:, n - 1:n], carry_ref.shape)
    c2 = c * LOG2E
    c_ref[...] = c2
    hi, mid, lo = _split3(-c2)
    parts = jnp.concatenate([hi, mid, lo, jnp.ones_like(hi)], axis=0)
    bias_lanes = lax.dot_general(parts, place_ref[...], _CONTRACT_FIRST,
                                 preferred_element_type=F32)
    k_ref[...] = (_dot(hn, wk_ref[...]) + bias_lanes).astype(BF16)


def _kv_placement(heads):
    place = jnp.zeros((4 * heads, heads * LANES), F32)
    h = jnp.arange(heads)
    for part in range(AUG_PARTS):
        place = place.at[part * heads + h, h * LANES + AUG_KSIDE_C + part].set(1.0)
        place = place.at[AUG_PARTS * heads, h * LANES + AUG_QSIDE_C + part].set(1.0)
    return place.astype(BF16)


def _shared_kv(x3d, g, w_kv, b_f):
    batch, seq, d = x3d.shape
    heads = b_f.shape[0]
    hd = d // heads
    wk = jnp.pad(w_kv[:, :d].reshape(d, heads, hd), ((0, 0), (0, 0), (0, LANES - hd)))
    wk = wk.reshape(d, heads * LANES).astype(BF16)
    wvt = w_kv[:, d:2 * d].T.astype(BF16)
    wft = w_kv[:, 2 * d:].T.astype(BF16)
    kw = heads * LANES
    return pl.pallas_call(
        _kv_kernel,
        grid=(batch, seq // KV_TILE),
        in_specs=[pl.BlockSpec((None, KV_TILE, d), lambda b, t: (b, t, 0)),
                  _resident((1, d)), _resident((d, kw)), _resident((d, d)),
                  _resident((heads, d)), _resident((heads, 1)), _resident((4 * heads, kw))],
        out_specs=[pl.BlockSpec((None, KV_TILE, kw), lambda b, t: (b, t, 0)),
                   pl.BlockSpec((None, d, KV_TILE), lambda b, t: (b, 0, t)),
                   pl.BlockSpec((None, heads, KV_TILE), lambda b, t: (b, 0, t))],
        out_shape=[jax.ShapeDtypeStruct((batch, seq, kw), BF16),
                   jax.ShapeDtypeStruct((batch, d, seq), BF16),
                   jax.ShapeDtypeStruct((batch, heads, seq), F32)],
        scratch_shapes=[pltpu.VMEM((heads, LANES), F32)],
        compiler_params=pltpu.CompilerParams(dimension_semantics=("parallel", "arbitrary"),
                                             vmem_limit_bytes=40 * MIB),
        name="shared_kv",
    )(x3d, g.reshape(1, d), wk, wvt, wft, b_f.reshape(heads, 1), _kv_placement(heads))


def _q_kernel(x_ref, g_ref, wqt_ref, c_ref, qt_ref):
    hd = FOX_HEAD_DIM
    hn = _rmsnorm(x_ref[...], g_ref[...]).astype(BF16)
    qt = lax.dot_general(wqt_ref[...], hn, _CONTRACT_LAST, preferred_element_type=F32)
    qt = (qt * (hd ** -0.5 * LOG2E)).astype(BF16)
    hi, mid, lo = (p.astype(F32) for p in _split3(c_ref[...]))
    n = hn.shape[0]
    r = lax.broadcasted_iota(jnp.int32, (BF16_SUBLANES, n), 0)
    for h in range(c_ref.shape[0]):
        base = h * LANES
        bias = jnp.where(r == 0, hi[h:h + 1], jnp.where(r == 1, mid[h:h + 1], jnp.where(
            r == 2, lo[h:h + 1], jnp.where(r < 2 * AUG_PARTS, 1.0, 0.0))))
        qt_ref[base:base + hd, :] = qt[h * hd:(h + 1) * hd, :]
        qt_ref[base + hd:base + hd + BF16_SUBLANES, :] = bias.astype(BF16)
        qt_ref[base + hd + BF16_SUBLANES:base + LANES, :] = jnp.zeros(
            (LANES - hd - BF16_SUBLANES, n), BF16)


def _q_proj(x3d, g, w_q, c2):
    batch, seq, d = x3d.shape
    heads = c2.shape[1]
    return pl.pallas_call(
        _q_kernel,
        grid=(batch, seq // ROW_TILE),
        in_specs=[pl.BlockSpec((None, ROW_TILE, d), lambda b, t: (b, t, 0)),
                  _resident((1, d)), _resident((d, d)),
                  pl.BlockSpec((None, heads, ROW_TILE), lambda b, t: (b, 0, t))],
        out_specs=pl.BlockSpec((None, heads * LANES, ROW_TILE), lambda b, t: (b, 0, t)),
        out_shape=jax.ShapeDtypeStruct((batch, heads * LANES, seq), BF16),
        compiler_params=pltpu.CompilerParams(dimension_semantics=("parallel", "parallel"),
                                             vmem_limit_bytes=40 * MIB),
        name="fox_q_proj",
    )(x3d, g.reshape(1, d), w_q.T.astype(BF16), c2)


def _fox_kernel(qt_ref, k_ref, vt_ref, o_ref, s_ref, m_ref, acc_ref, *, n_q):
    t = FOX_TILE
    hd = FOX_HEAD_DIM
    heads = range(FOX_HEADS_PER_STEP)
    causal = (lax.broadcasted_iota(jnp.int32, (t, t), 0)
              <= lax.broadcasted_iota(jnp.int32, (t, t), 1))
    ones = jnp.ones((BF16_SUBLANES, t), BF16)

    def logits(qi, j, slot):
        keys = pl.ds(j * t, t)
        queries = pl.ds(qi * t, t)
        for h in heads:
            s_ref[slot, h] = _dot(k_ref[keys, h * LANES:(h + 1) * LANES],
                                  qt_ref[h * LANES:(h + 1) * LANES, queries])

    def update(qi, j, slot, diagonal):
        keys = pl.ds(j * t, t)

        def tile(h):
            s = s_ref[slot, h]
            return jnp.where(causal, s, NEG_BIG) if diagonal else s

        if diagonal:
            m_new = [jnp.max(tile(h), axis=0, keepdims=True) for h in heads]
        else:
            m_old = [m_ref[qi, h] for h in heads]
            m_new = [jnp.maximum(m_old[h], jnp.max(tile(h), axis=0, keepdims=True)) for h in heads]
            alpha = [jnp.exp2(m_old[h] - m_new[h]) for h in heads]
        p = [jnp.exp2(tile(h) - m_new[h]).astype(BF16) for h in heads]
        pv = [_dot(jnp.concatenate([vt_ref[h * hd:(h + 1) * hd, keys], ones], axis=0), p[h])
              for h in heads]
        for h in heads:
            m_ref[qi, h] = m_new[h]
            acc_ref[qi, h] = pv[h] if diagonal else alpha[h] * acc_ref[qi, h] + pv[h]

    ns, ahead = FOX_SLOTS, FOX_LOOKAHEAD
    tiles = [(qi, qi) for qi in range(n_q)]
    tiles += [(qi, j) for j in range(n_q) for qi in range(j + 1, n_q)]
    for w in range(ahead):
        logits(*tiles[w], w)
    for k, (qi, j) in enumerate(tiles):
        if k + ahead < len(tiles):
            logits(*tiles[k + ahead], (k + ahead) % ns)
        update(qi, j, k % ns, qi == j)

    for qi in range(n_q):
        o_t = jnp.concatenate([acc_ref[qi, h, :hd] / acc_ref[qi, h, hd:hd + 1] for h in heads],
                              axis=0)
        o_ref[pl.ds(qi * t, t), :] = o_t.T.astype(BF16)


def _fox_attention(qt, k_aug, vt, batch, seq, d):
    nh = FOX_HEADS_PER_STEP
    groups = d // (nh * FOX_HEAD_DIM)
    t = FOX_TILE
    out = pl.pallas_call(
        functools.partial(_fox_kernel, n_q=seq // t),
        grid=(batch, groups),
        in_specs=[pl.BlockSpec((None, nh * LANES, seq), lambda b, p: (b, p, 0)),
                  pl.BlockSpec((None, seq, nh * LANES), lambda b, p: (b, 0, p)),
                  pl.BlockSpec((None, nh * FOX_HEAD_DIM, seq), lambda b, p: (b, p, 0))],
        out_specs=pl.BlockSpec((None, seq, nh * FOX_HEAD_DIM), lambda b, p: (b, 0, p)),
        out_shape=jax.ShapeDtypeStruct((batch, seq, d), BF16),
        scratch_shapes=[pltpu.VMEM((FOX_SLOTS, nh, t, t), F32),
                        pltpu.VMEM((seq // t, nh, 1, t), F32),
                        pltpu.VMEM((seq // t, nh, FOX_HEAD_DIM + BF16_SUBLANES, t), F32)],
        compiler_params=pltpu.CompilerParams(
            dimension_semantics=("parallel", "parallel"),
            vmem_limit_bytes=40 * MIB),
        name="fox_attention",
    )(qt, k_aug, vt)
    return out.reshape(batch * seq, d)


def kernel(x, attn_norm, ffn_norm, gla_w_in, gla_w_alpha_up, gla_b_alpha, gla_g_head, gla_w_out,
           kv_norm, w_kv, b_f, fox_w_q, fox_w_o, ffn_w_gu, ffn_w_down, final_norm):
    batch, seq, d = x.shape
    depth = attn_norm.shape[0]
    n_gla = gla_w_in.shape[0]
    assert (batch * seq) % ROW_TILE == 0 and seq % GLA_TIME_TILE == 0 and seq % ROW_TILE == 0
    assert seq % KV_TILE == 0 and seq % FOX_TILE == 0
    assert (FOX_HEADS_PER_STEP * FOX_HEAD_DIM) % LANES == 0
    assert d // b_f.shape[0] == FOX_HEAD_DIM

    h = x.reshape(batch * seq, d)
    k_aug = v_t = c2 = None
    for layer in range(depth):
        if layer < n_gla:
            q, k, v, r, la = _gla_in_proj(h, attn_norm[layer], gla_w_in[layer],
                                          gla_w_alpha_up[layer], gla_b_alpha[layer])
            o = _gla_recurrence(q, k, v, r, la, gla_g_head[layer], batch, seq)
            w_o = gla_w_out[layer]
        else:
            h3 = h.reshape(batch, seq, d)
            if layer == n_gla:
                k_aug, v_t, c2 = _shared_kv(h3, kv_norm, w_kv, b_f)
            j = layer - n_gla
            q_t = _q_proj(h3, attn_norm[layer], fox_w_q[j], c2)
            o = _fox_attention(q_t, k_aug, v_t, batch, seq, d)
            w_o = fox_w_o[j]
        h = _out_ffn(h, o, w_o, ffn_norm[layer], ffn_w_gu[layer], ffn_w_down[layer],
                     final_norm, final=(layer == depth - 1))
    return h.reshape(batch, seq, d)
```

```python
import functools
import math

import jax
import jax.numpy as jnp
from jax import lax
from jax.experimental import pallas as pl
from jax.experimental.pallas import tpu as pltpu

F32 = jnp.float32
BF16 = jnp.bfloat16
HIGHEST = lax.Precision.HIGHEST

EPS = 1e-6
GLA_HEADS = 4
GLA_RANK = 16
GLA_TAU = 16.0
GLA_CHUNK = 64
FOX_HEAD_DIM = 64
LOG2E = math.log2(math.e)

LANES = 128
BF16_SUBLANES = 16
MIB = 1024 * 1024

ROW_TILE = 1024
FFN_ROW_TILE = 1024
GLA_TIME_TILE = 1024
KV_TILE = 1024
FOX_TILE = 256
FOX_HEADS_PER_STEP = 2
FOX_SLOTS = 4
FOX_LOOKAHEAD = 2
FFN_CHUNKS = 11
NEG_BIG = -1e30

_CONTRACT_LAST = (((1,), (1,)), ((), ()))
_CONTRACT_FIRST = (((0,), (0,)), ((), ()))


def _rmsnorm(x, g):
    return x * lax.rsqrt(jnp.mean(x * x, axis=-1, keepdims=True) + EPS) * g


def _log_sigmoid(z):
    return jnp.minimum(z, 0.0) - jnp.log(1.0 + jnp.exp(-jnp.abs(z)))


def _silu(x):
    return x * jax.nn.sigmoid(x)


def _dot(a, b):
    return jnp.dot(a, b, preferred_element_type=F32)


def _resident(shape):
    return pl.BlockSpec(shape, lambda *_: (0,) * len(shape), pipeline_mode=pl.Buffered(1))


def _gla_in_kernel(x_ref, g_ref, wq_ref, wk_ref, wv_ref, wr_ref, wa_ref, wup_ref, ba_ref,
                   q_ref, k_ref, v_ref, r_ref, la_ref):
    hn = _rmsnorm(x_ref[...], g_ref[...]).astype(BF16)
    q_ref[...] = _dot(hn, wq_ref[...]).astype(BF16)
    k_ref[...] = _dot(hn, wk_ref[...]).astype(BF16)
    v_ref[...] = _dot(hn, wv_ref[...]).astype(BF16)
    r_ref[...] = _dot(hn, wr_ref[...]).astype(BF16)
    a_low = _dot(hn, wa_ref[...])
    z = _dot(a_low.astype(BF16), wup_ref[...]) + ba_ref[...]
    la_ref[...] = _log_sigmoid(z) * (1.0 / GLA_TAU)


def _gla_in_proj(x2d, g, w_in, w_up, b_alpha):
    m, d = x2d.shape
    qk = w_up.shape[1]
    vdim = (w_in.shape[1] - 2 * qk - GLA_RANK) // 2
    wq = w_in[:, :qk].astype(BF16)
    wk = w_in[:, qk:2 * qk].astype(BF16)
    wv = w_in[:, 2 * qk:2 * qk + vdim].astype(BF16)
    wr = w_in[:, 2 * qk + vdim:2 * qk + 2 * vdim].astype(BF16)
    wa = jnp.pad(w_in[:, 2 * qk + 2 * vdim:], ((0, 0), (0, LANES - GLA_RANK))).astype(BF16)
    wup = jnp.pad(w_up, ((0, LANES - GLA_RANK), (0, 0))).astype(BF16)
    row = lambda n: pl.BlockSpec((ROW_TILE, n), lambda i: (i, 0))
    return pl.pallas_call(
        _gla_in_kernel,
        grid=(m // ROW_TILE,),
        in_specs=[row(d), _resident((1, d)), _resident((d, qk)), _resident((d, qk)),
                  _resident((d, vdim)), _resident((d, vdim)), _resident((d, LANES)),
                  _resident((LANES, qk)), _resident((1, qk))],
        out_specs=[row(qk), row(qk), row(vdim), row(vdim), row(qk)],
        out_shape=[jax.ShapeDtypeStruct((m, qk), BF16), jax.ShapeDtypeStruct((m, qk), BF16),
                   jax.ShapeDtypeStruct((m, vdim), BF16), jax.ShapeDtypeStruct((m, vdim), BF16),
                   jax.ShapeDtypeStruct((m, qk), F32)],
        compiler_params=pltpu.CompilerParams(dimension_semantics=("parallel",),
                                             vmem_limit_bytes=40 * MIB),
        name="gla_in_proj",
    )(x2d, g.reshape(1, d), wq, wk, wv, wr, wa, wup, b_alpha.reshape(1, qk))


def _gla_kernel(q_ref, k_ref, v_ref, r_ref, la_ref, g_ref, o_ref, st_ref, *, dk, dv, n_chunks):
    c = GLA_CHUNK

    @pl.when(pl.program_id(1) == 0)
    def _():
        st_ref[...] = jnp.zeros_like(st_ref)

    causal = (lax.broadcasted_iota(jnp.int32, (c, c), 0)
              >= lax.broadcasted_iota(jnp.int32, (c, c), 1))
    qk = GLA_HEADS * dk
    row = lax.broadcasted_iota(jnp.int32, (c, qk), 0)
    g = g_ref[...]
    scale = dk ** -0.5
    heads = range(GLA_HEADS)
    kcols = [slice(h * dk, (h + 1) * dk) for h in heads]
    vcols = [slice(h * dv, (h + 1) * dv) for h in heads]

    for n in range(n_chunks):
        rows = pl.ds(n * c, c)
        b = la_ref[rows, :]
        shift = 1
        while shift < c:
            b = b + jnp.where(row >= shift, pltpu.roll(b, shift, axis=0), 0.0)
            shift *= 2
        b_last = b[c - 1:c, :]
        qf = q_ref[rows, :].astype(F32) * scale
        kf = k_ref[rows, :].astype(F32)
        q_dec = (qf * jnp.exp(b)).astype(BF16)
        k_dec = (kf * jnp.exp(-b)).astype(BF16)
        k_out = (kf * jnp.exp(b_last - b)).astype(BF16)
        decay = jnp.exp(b_last)
        v = [v_ref[rows, vcols[h]] for h in heads]
        a = [lax.dot_general(q_dec[:, kcols[h]], k_dec[:, kcols[h]], _CONTRACT_LAST,
                             preferred_element_type=F32) for h in heads]
        a = [jnp.where(causal, a[h], 0.0).astype(BF16) for h in heads]
        st = [st_ref[h] for h in heads]
        o = [_dot(a[h], v[h]) + lax.dot_general(q_dec[:, kcols[h]], st[h].astype(BF16),
                                                _CONTRACT_LAST, preferred_element_type=F32)
             for h in heads]
        kv = [lax.dot_general(v[h], k_out[:, kcols[h]], _CONTRACT_FIRST,
                              preferred_element_type=F32) for h in heads]
        for h in heads:
            st_ref[h] = st[h] * decay[:, kcols[h]] + kv[h]
            gate = _silu(r_ref[rows, vcols[h]].astype(F32))
            o_ref[rows, vcols[h]] = (_rmsnorm(o[h], g) * gate).astype(BF16)


def _gla_recurrence(q, k, v, r, la, g_head, batch, seq):
    qk = q.shape[-1]
    vdim = v.shape[-1]
    dk, dv = qk // GLA_HEADS, vdim // GLA_HEADS
    tt = GLA_TIME_TILE
    shp = lambda a: a.reshape(batch, seq, a.shape[-1])
    spec = lambda n: pl.BlockSpec((None, tt, n), lambda b, t: (b, t, 0))
    out = pl.pallas_call(
        functools.partial(_gla_kernel, dk=dk, dv=dv, n_chunks=tt // GLA_CHUNK),
        grid=(batch, seq // tt),
        in_specs=[spec(qk), spec(qk), spec(vdim), spec(vdim), spec(qk), _resident((1, dv))],
        out_specs=spec(vdim),
        out_shape=jax.ShapeDtypeStruct((batch, seq, vdim), BF16),
        scratch_shapes=[pltpu.VMEM((GLA_HEADS, dv, dk), F32)],
        compiler_params=pltpu.CompilerParams(dimension_semantics=("parallel", "arbitrary"),
                                             vmem_limit_bytes=40 * MIB),
        name="gla_recurrence",
    )(shp(q), shp(k), shp(v), shp(r), shp(la), g_head.reshape(1, dv))
    return out.reshape(batch * seq, vdim)


def _out_ffn_kernel(x_ref, o_ref, wo_ref, g_ref, wg_ref, wu_ref, wd_ref, gf_ref, y_ref, *, final):
    x1 = x_ref[...] + _dot(o_ref[...], wo_ref[...])
    hn = _rmsnorm(x1, g_ref[...]).astype(BF16)
    dff = wg_ref.shape[1]
    fc = dff // FFN_CHUNKS
    acc = x1
    for ci in range(FFN_CHUNKS):
        cols = slice(ci * fc, (ci + 1) * fc)
        gate = _dot(hn, wg_ref[:, cols])
        up = _dot(hn, wu_ref[:, cols])
        acc = acc + _dot((_silu(gate) * up).astype(BF16), wd_ref[cols, :])
    if final:
        acc = _rmsnorm(acc, gf_ref[...])
    y_ref[...] = acc


def _out_ffn(x2d, o2d, w_o, g_ffn, w_gu, w_down, g_final, final):
    m, d = x2d.shape
    dff = w_down.shape[0]
    assert dff % (FFN_CHUNKS * LANES) == 0
    wg = w_gu[:, :dff].astype(BF16)
    wu = w_gu[:, dff:].astype(BF16)
    row = lambda: pl.BlockSpec((FFN_ROW_TILE, d), lambda i: (i, 0))
    return pl.pallas_call(
        functools.partial(_out_ffn_kernel, final=final),
        grid=(m // FFN_ROW_TILE,),
        in_specs=[row(), row(), _resident((d, d)), _resident((1, d)), _resident((d, dff)),
                  _resident((d, dff)), _resident((dff, d)), _resident((1, d))],
        out_specs=row(),
        out_shape=jax.ShapeDtypeStruct((m, d), F32),
        compiler_params=pltpu.CompilerParams(dimension_semantics=("parallel",),
                                             vmem_limit_bytes=52 * MIB),
        name="out_proj_ffn",
    )(x2d, o2d, w_o.astype(BF16), g_ffn.reshape(1, d), wg, wu, w_down.astype(BF16),
      g_final.reshape(1, d))


AUG_QSIDE_C = FOX_HEAD_DIM
AUG_KSIDE_C = FOX_HEAD_DIM + 3
AUG_PARTS = 3


def _split3(x):
    hi = x.astype(BF16)
    r1 = x - hi.astype(F32)
    mid = r1.astype(BF16)
    lo = (r1 - mid.astype(F32)).astype(BF16)
    return hi, mid, lo


def _kv_kernel(x_ref, g_ref, wk_ref, wvt_ref, wft_ref, bf_ref, place_ref,
               k_ref, vt_ref, c_ref, carry_ref):
    @pl.when(pl.program_id(1) == 0)
    def _():
        carry_ref[...] = jnp.zeros_like(carry_ref)

    hn = _rmsnorm(x_ref[...], g_ref[...]).astype(BF16)
    vt_ref[...] = lax.dot_general(wvt_ref[...], hn, _CONTRACT_LAST,
                                  preferred_element_type=F32).astype(BF16)
    f_t = lax.dot_general(wft_ref[...], hn, _CONTRACT_LAST, preferred_element_type=F32)
    log_f = _log_sigmoid(f_t + bf_ref[...])
    n = log_f.shape[1]
    upper = (lax.broadcasted_iota(jnp.int32, (n, n), 0)
             <= lax.broadcasted_iota(jnp.int32, (n, n), 1)).astype(F32)
    c = jnp.dot(log_f, upper, precision=HIGHEST, preferred_element_type=F32) + carry_ref[:, 0:1]
    carry_ref[...] = jnp.broadcast_to(c[:, n - 1:n], carry_ref.shape)
    c2 = c * LOG2E
    c_ref[...] = c2
    hi, mid, lo = _split3(-c2)
    parts = jnp.concatenate([hi, mid, lo, jnp.ones_like(hi)], axis=0)
    bias_lanes = lax.dot_general(parts, place_ref[...], _CONTRACT_FIRST,
                                 preferred_element_type=F32)
    k = _dot(hn, wk_ref[...])
    hd = FOX_HEAD_DIM
    low = lax.broadcasted_iota(jnp.int32, (n, LANES), 1) < hd
    for pair in range(k.shape[1] // LANES):
        kp = k[:, pair * LANES:(pair + 1) * LANES]
        for odd, src in enumerate((kp, pltpu.roll(kp, hd, axis=1))):
            cols = slice((2 * pair + odd) * LANES, (2 * pair + odd + 1) * LANES)
            k_ref[:, cols] = jnp.where(low, src, bias_lanes[:, cols]).astype(BF16)


def _kv_placement(heads):
    place = jnp.zeros((4 * heads, heads * LANES), F32)
    h = jnp.arange(heads)
    for part in range(AUG_PARTS):
        place = place.at[part * heads + h, h * LANES + AUG_KSIDE_C + part].set(1.0)
        place = place.at[AUG_PARTS * heads, h * LANES + AUG_QSIDE_C + part].set(1.0)
    return place.astype(BF16)


def _shared_kv(x3d, g, w_kv, b_f):
    batch, seq, d = x3d.shape
    heads = b_f.shape[0]
    wk = w_kv[:, :d].astype(BF16)
    wvt = w_kv[:, d:2 * d].T.astype(BF16)
    wft = w_kv[:, 2 * d:].T.astype(BF16)
    kw = heads * LANES
    return pl.pallas_call(
        _kv_kernel,
        grid=(batch, seq // KV_TILE),
        in_specs=[pl.BlockSpec((None, KV_TILE, d), lambda b, t: (b, t, 0)),
                  _resident((1, d)), _resident((d, d)), _resident((d, d)),
                  _resident((heads, d)), _resident((heads, 1)), _resident((4 * heads, kw))],
        out_specs=[pl.BlockSpec((None, KV_TILE, kw), lambda b, t: (b, t, 0)),
                   pl.BlockSpec((None, d, KV_TILE), lambda b, t: (b, 0, t)),
                   pl.BlockSpec((None, heads, KV_TILE), lambda b, t: (b, 0, t))],
        out_shape=[jax.ShapeDtypeStruct((batch, seq, kw), BF16),
                   jax.ShapeDtypeStruct((batch, d, seq), BF16),
                   jax.ShapeDtypeStruct((batch, heads, seq), F32)],
        scratch_shapes=[pltpu.VMEM((heads, LANES), F32)],
        compiler_params=pltpu.CompilerParams(dimension_semantics=("parallel", "arbitrary"),
                                             vmem_limit_bytes=40 * MIB),
        name="shared_kv",
    )(x3d, g.reshape(1, d), wk, wvt, wft, b_f.reshape(heads, 1), _kv_placement(heads))


def _q_kernel(x_ref, g_ref, wqt_ref, c_ref, qt_ref):
    hd = FOX_HEAD_DIM
    hn = _rmsnorm(x_ref[...], g_ref[...]).astype(BF16)
    qt = lax.dot_general(wqt_ref[...], hn, _CONTRACT_LAST, preferred_element_type=F32)
    qt = (qt * (hd ** -0.5 * LOG2E)).astype(BF16)
    hi, mid, lo = (p.astype(F32) for p in _split3(c_ref[...]))
    n = hn.shape[0]
    r = lax.broadcasted_iota(jnp.int32, (BF16_SUBLANES, n), 0)
    for h in range(c_ref.shape[0]):
        base = h * LANES
        bias = jnp.where(r == 0, hi[h:h + 1], jnp.where(r == 1, mid[h:h + 1], jnp.where(
            r == 2, lo[h:h + 1], jnp.where(r < 2 * AUG_PARTS, 1.0, 0.0))))
        qt_ref[base:base + hd, :] = qt[h * hd:(h + 1) * hd, :]
        qt_ref[base + hd:base + hd + BF16_SUBLANES, :] = bias.astype(BF16)
        qt_ref[base + hd + BF16_SUBLANES:base + LANES, :] = jnp.zeros(
            (LANES - hd - BF16_SUBLANES, n), BF16)


def _q_proj(x3d, g, w_q, c2):
    batch, seq, d = x3d.shape
    heads = c2.shape[1]
    return pl.pallas_call(
        _q_kernel,
        grid=(batch, seq // ROW_TILE),
        in_specs=[pl.BlockSpec((None, ROW_TILE, d), lambda b, t: (b, t, 0)),
                  _resident((1, d)), _resident((d, d)),
                  pl.BlockSpec((None, heads, ROW_TILE), lambda b, t: (b, 0, t))],
        out_specs=pl.BlockSpec((None, heads * LANES, ROW_TILE), lambda b, t: (b, 0, t)),
        out_shape=jax.ShapeDtypeStruct((batch, heads * LANES, seq), BF16),
        compiler_params=pltpu.CompilerParams(dimension_semantics=("parallel", "parallel"),
                                             vmem_limit_bytes=40 * MIB),
        name="fox_q_proj",
    )(x3d, g.reshape(1, d), w_q.T.astype(BF16), c2)


def _fox_kernel(qt_ref, k_ref, vt_ref, o_ref, s_ref, m_ref, acc_ref, *, n_q):
    t = FOX_TILE
    hd = FOX_HEAD_DIM
    heads = range(FOX_HEADS_PER_STEP)
    causal = (lax.broadcasted_iota(jnp.int32, (t, t), 0)
              <= lax.broadcasted_iota(jnp.int32, (t, t), 1))
    ones = jnp.ones((BF16_SUBLANES, t), BF16)

    def logits(qi, j, slot):
        keys = pl.ds(j * t, t)
        queries = pl.ds(qi * t, t)
        for h in heads:
            s_ref[slot, h] = _dot(k_ref[keys, h * LANES:(h + 1) * LANES],
                                  qt_ref[h * LANES:(h + 1) * LANES, queries])

    def update(qi, j, slot, diagonal):
        keys = pl.ds(j * t, t)

        def tile(h):
            s = s_ref[slot, h]
            return jnp.where(causal, s, NEG_BIG) if diagonal else s

        if diagonal:
            m_new = [jnp.max(tile(h), axis=0, keepdims=True) for h in heads]
        else:
            m_old = [m_ref[qi, h] for h in heads]
            m_new = [jnp.maximum(m_old[h], jnp.max(tile(h), axis=0, keepdims=True)) for h in heads]
            alpha = [jnp.exp2(m_old[h] - m_new[h]) for h in heads]
        p = [jnp.exp2(tile(h) - m_new[h]).astype(BF16) for h in heads]
        pv = [_dot(jnp.concatenate([vt_ref[h * hd:(h + 1) * hd, keys], ones], axis=0), p[h])
              for h in heads]
        for h in heads:
            m_ref[qi, h] = m_new[h]
            acc_ref[qi, h] = pv[h] if diagonal else alpha[h] * acc_ref[qi, h] + pv[h]

    ns, ahead = FOX_SLOTS, FOX_LOOKAHEAD
    tiles = [(qi, qi) for qi in range(n_q)]
    tiles += [(qi, j) for j in range(n_q) for qi in range(j + 1, n_q)]
    for w in range(ahead):
        logits(*tiles[w], w)
    for k, (qi, j) in enumerate(tiles):
        if k + ahead < len(tiles):
            logits(*tiles[k + ahead], (k + ahead) % ns)
        update(qi, j, k % ns, qi == j)

    for qi in range(n_q):
        o_t = jnp.concatenate([acc_ref[qi, h, :hd] / acc_ref[qi, h, hd:hd + 1] for h in heads],
                              axis=0)
        o_ref[pl.ds(qi * t, t), :] = o_t.T.astype(BF16)


def _fox_attention(qt, k_aug, vt, batch, seq, d):
    nh = FOX_HEADS_PER_STEP
    groups = d // (nh * FOX_HEAD_DIM)
    t = FOX_TILE
    out = pl.pallas_call(
        functools.partial(_fox_kernel, n_q=seq // t),
        grid=(batch, groups),
        in_specs=[pl.BlockSpec((None, nh * LANES, seq), lambda b, p: (b, p, 0)),
                  pl.BlockSpec((None, seq, nh * LANES), lambda b, p: (b, 0, p)),
                  pl.BlockSpec((None, nh * FOX_HEAD_DIM, seq), lambda b, p: (b, p, 0))],
        out_specs=pl.BlockSpec((None, seq, nh * FOX_HEAD_DIM), lambda b, p: (b, 0, p)),
        out_shape=jax.ShapeDtypeStruct((batch, seq, d), BF16),
        scratch_shapes=[pltpu.VMEM((FOX_SLOTS, nh, t, t), F32),
                        pltpu.VMEM((seq // t, nh, 1, t), F32),
                        pltpu.VMEM((seq // t, nh, FOX_HEAD_DIM + BF16_SUBLANES, t), F32)],
        compiler_params=pltpu.CompilerParams(
            dimension_semantics=("parallel", "parallel"),
            vmem_limit_bytes=40 * MIB),
        name="fox_attention",
    )(qt, k_aug, vt)
    return out.reshape(batch * seq, d)


def kernel(x, attn_norm, ffn_norm, gla_w_in, gla_w_alpha_up, gla_b_alpha, gla_g_head, gla_w_out,
           kv_norm, w_kv, b_f, fox_w_q, fox_w_o, ffn_w_gu, ffn_w_down, final_norm):
    batch, seq, d = x.shape
    depth = attn_norm.shape[0]
    n_gla = gla_w_in.shape[0]
    assert (batch * seq) % ROW_TILE == 0 and seq % GLA_TIME_TILE == 0 and seq % ROW_TILE == 0
    assert seq % KV_TILE == 0 and seq % FOX_TILE == 0
    assert (FOX_HEADS_PER_STEP * FOX_HEAD_DIM) % LANES == 0
    assert d // b_f.shape[0] == FOX_HEAD_DIM

    h = x.reshape(batch * seq, d)
    k_aug = v_t = c2 = None
    for layer in range(depth):
        if layer < n_gla:
            q, k, v, r, la = _gla_in_proj(h, attn_norm[layer], gla_w_in[layer],
                                          gla_w_alpha_up[layer], gla_b_alpha[layer])
            o = _gla_recurrence(q, k, v, r, la, gla_g_head[layer], batch, seq)
            w_o = gla_w_out[layer]
        else:
            h3 = h.reshape(batch, seq, d)
            if layer == n_gla:
                k_aug, v_t, c2 = _shared_kv(h3, kv_norm, w_kv, b_f)
            j = layer - n_gla
            q_t = _q_proj(h3, attn_norm[layer], fox_w_q[j], c2)
            o = _fox_attention(q_t, k_aug, v_t, batch, seq, d)
            w_o = fox_w_o[j]
        h = _out_ffn(h, o, w_o, ffn_norm[layer], ffn_w_gu[layer], ffn_w_down[layer],
                     final_norm, final=(layer == depth - 1))
    return h.reshape(batch, seq, d)
```

```python
import functools
import math

import jax
import jax.numpy as jnp
from jax import lax
from jax.experimental import pallas as pl
from jax.experimental.pallas import tpu as pltpu

F32 = jnp.float32
BF16 = jnp.bfloat16
HIGHEST = lax.Precision.HIGHEST

EPS = 1e-6
GLA_HEADS = 4
GLA_RANK = 16
GLA_TAU = 16.0
GLA_CHUNK = 64
FOX_HEAD_DIM = 64
LOG2E = math.log2(math.e)

LANES = 128
BF16_SUBLANES = 16
MIB = 1024 * 1024

ROW_TILE = 1024
FFN_ROW_TILE = 1024
GLA_TIME_TILE = 1024
KV_TILE = 1024
FOX_TILE = 256
FOX_HEADS_PER_STEP = 2
FOX_SLOTS = 4
FOX_LOOKAHEAD = 2
FFN_CHUNKS = 11
NEG_BIG = -1e30

_CONTRACT_LAST = (((1,), (1,)), ((), ()))
_CONTRACT_FIRST = (((0,), (0,)), ((), ()))


def _rmsnorm(x, g):
    return x * lax.rsqrt(jnp.mean(x * x, axis=-1, keepdims=True) + EPS) * g


def _log_sigmoid(z):
    return jnp.minimum(z, 0.0) - jnp.log(1.0 + jnp.exp(-jnp.abs(z)))


def _silu(x):
    return x * jax.nn.sigmoid(x)


def _dot(a, b):
    return jnp.dot(a, b, preferred_element_type=F32)


def _resident(shape):
    return pl.BlockSpec(shape, lambda *_: (0,) * len(shape), pipeline_mode=pl.Buffered(1))


def _cast_job(stack, layer, steps):
    _, rows, cols = stack.shape
    nb = max(n for n in range(1, steps + 1)
             if rows % n == 0 and (rows // n) % BF16_SUBLANES == 0)
    br = rows // nb
    in_spec = pl.BlockSpec((None, br, cols), lambda i: (layer, jnp.minimum(i, nb - 1), 0))
    out_spec = pl.BlockSpec((br, cols), lambda i: (jnp.minimum(i, nb - 1), 0))
    return in_spec, out_spec, jax.ShapeDtypeStruct((rows, cols), BF16)


def _run_cast_jobs(src_refs, dst_refs):
    for src, dst in zip(src_refs, dst_refs):
        dst[...] = src[...].astype(BF16)


def _gla_in_kernel(x_ref, g_ref, wq_ref, wk_ref, wv_ref, wr_ref, wa_ref, wup_ref, ba_ref, *refs,
                   n_cast):
    cast_src, (q_ref, k_ref, v_ref, r_ref, la_ref), cast_dst = (
        refs[:n_cast], refs[n_cast:n_cast + 5], refs[n_cast + 5:])
    _run_cast_jobs(cast_src, cast_dst)
    hn = _rmsnorm(x_ref[...], g_ref[...]).astype(BF16)
    q_ref[...] = _dot(hn, wq_ref[...]).astype(BF16)
    k_ref[...] = _dot(hn, wk_ref[...]).astype(BF16)
    v_ref[...] = _dot(hn, wv_ref[...]).astype(BF16)
    r_ref[...] = _dot(hn, wr_ref[...]).astype(BF16)
    a_low = _dot(hn, wa_ref[...])
    z = _dot(a_low.astype(BF16), wup_ref[...]) + ba_ref[...]
    la_ref[...] = _log_sigmoid(z) * (1.0 / GLA_TAU)


def _gla_in_proj(x2d, g, w_in, w_up, b_alpha, cast_jobs=()):
    m, d = x2d.shape
    steps = m // ROW_TILE
    jobs = [_cast_job(stack, layer, steps) for stack, layer in cast_jobs]
    qk = w_up.shape[1]
    vdim = (w_in.shape[1] - 2 * qk - GLA_RANK) // 2
    wq = w_in[:, :qk].astype(BF16)
    wk = w_in[:, qk:2 * qk].astype(BF16)
    wv = w_in[:, 2 * qk:2 * qk + vdim].astype(BF16)
    wr = w_in[:, 2 * qk + vdim:2 * qk + 2 * vdim].astype(BF16)
    wa = jnp.pad(w_in[:, 2 * qk + 2 * vdim:], ((0, 0), (0, LANES - GLA_RANK))).astype(BF16)
    wup = jnp.pad(w_up, ((0, LANES - GLA_RANK), (0, 0))).astype(BF16)
    row = lambda n: pl.BlockSpec((ROW_TILE, n), lambda i: (i, 0))
    return pl.pallas_call(
        functools.partial(_gla_in_kernel, n_cast=len(jobs)),
        grid=(steps,),
        in_specs=[row(d), _resident((1, d)), _resident((d, qk)), _resident((d, qk)),
                  _resident((d, vdim)), _resident((d, vdim)), _resident((d, LANES)),
                  _resident((LANES, qk)), _resident((1, qk))] + [j[0] for j in jobs],
        out_specs=[row(qk), row(qk), row(vdim), row(vdim), row(qk)] + [j[1] for j in jobs],
        out_shape=[jax.ShapeDtypeStruct((m, qk), BF16), jax.ShapeDtypeStruct((m, qk), BF16),
                   jax.ShapeDtypeStruct((m, vdim), BF16), jax.ShapeDtypeStruct((m, vdim), BF16),
                   jax.ShapeDtypeStruct((m, qk), F32)] + [j[2] for j in jobs],
        compiler_params=pltpu.CompilerParams(dimension_semantics=("arbitrary",),
                                             vmem_limit_bytes=44 * MIB),
        name="gla_in_proj",
    )(x2d, g.reshape(1, d), wq, wk, wv, wr, wa, wup, b_alpha.reshape(1, qk),
      *[stack for stack, _ in cast_jobs])


def _gla_kernel(q_ref, k_ref, v_ref, r_ref, la_ref, g_ref, o_ref, st_ref, *, dk, dv, n_chunks):
    c = GLA_CHUNK

    @pl.when(pl.program_id(1) == 0)
    def _():
        st_ref[...] = jnp.zeros_like(st_ref)

    causal = (lax.broadcasted_iota(jnp.int32, (c, c), 0)
              >= lax.broadcasted_iota(jnp.int32, (c, c), 1))
    qk = GLA_HEADS * dk
    row = lax.broadcasted_iota(jnp.int32, (c, qk), 0)
    g = g_ref[...]
    scale = dk ** -0.5
    heads = range(GLA_HEADS)
    kcols = [slice(h * dk, (h + 1) * dk) for h in heads]
    vcols = [slice(h * dv, (h + 1) * dv) for h in heads]

    for n in range(n_chunks):
        rows = pl.ds(n * c, c)
        b = la_ref[rows, :]
        shift = 1
        while shift < c:
            b = b + jnp.where(row >= shift, pltpu.roll(b, shift, axis=0), 0.0)
            shift *= 2
        b_last = b[c - 1:c, :]
        qf = q_ref[rows, :].astype(F32) * scale
        kf = k_ref[rows, :].astype(F32)
        q_dec = (qf * jnp.exp(b)).astype(BF16)
        k_dec = (kf * jnp.exp(-b)).astype(BF16)
        k_out = (kf * jnp.exp(b_last - b)).astype(BF16)
        decay = jnp.exp(b_last)
        v = [v_ref[rows, vcols[h]] for h in heads]
        a = [lax.dot_general(q_dec[:, kcols[h]], k_dec[:, kcols[h]], _CONTRACT_LAST,
                             preferred_element_type=F32) for h in heads]
        a = [jnp.where(causal, a[h], 0.0).astype(BF16) for h in heads]
        st = [st_ref[h] for h in heads]
        o = [_dot(a[h], v[h]) + lax.dot_general(q_dec[:, kcols[h]], st[h].astype(BF16),
                                                _CONTRACT_LAST, preferred_element_type=F32)
             for h in heads]
        kv = [lax.dot_general(v[h], k_out[:, kcols[h]], _CONTRACT_FIRST,
                              preferred_element_type=F32) for h in heads]
        for h in heads:
            st_ref[h] = st[h] * decay[:, kcols[h]] + kv[h]
            gate = _silu(r_ref[rows, vcols[h]].astype(F32))
            o_ref[rows, vcols[h]] = (_rmsnorm(o[h], g) * gate).astype(BF16)


def _gla_recurrence(q, k, v, r, la, g_head, batch, seq):
    qk = q.shape[-1]
    vdim = v.shape[-1]
    dk, dv = qk // GLA_HEADS, vdim // GLA_HEADS
    tt = GLA_TIME_TILE
    shp = lambda a: a.reshape(batch, seq, a.shape[-1])
    spec = lambda n: pl.BlockSpec((None, tt, n), lambda b, t: (b, t, 0))
    out = pl.pallas_call(
        functools.partial(_gla_kernel, dk=dk, dv=dv, n_chunks=tt // GLA_CHUNK),
        grid=(batch, seq // tt),
        in_specs=[spec(qk), spec(qk), spec(vdim), spec(vdim), spec(qk), _resident((1, dv))],
        out_specs=spec(vdim),
        out_shape=jax.ShapeDtypeStruct((batch, seq, vdim), BF16),
        scratch_shapes=[pltpu.VMEM((GLA_HEADS, dv, dk), F32)],
        compiler_params=pltpu.CompilerParams(dimension_semantics=("parallel", "arbitrary"),
                                             vmem_limit_bytes=40 * MIB),
        name="gla_recurrence",
    )(shp(q), shp(k), shp(v), shp(r), shp(la), g_head.reshape(1, dv))
    return out.reshape(batch * seq, vdim)


def _out_ffn_kernel(x_ref, o_ref, wo_ref, g_ref, wgu_ref, wd_ref, gf_ref, *refs, final, n_cast):
    cast_src, y_ref, cast_dst = refs[:n_cast], refs[n_cast], refs[n_cast + 1:]
    _run_cast_jobs(cast_src, cast_dst)
    x1 = x_ref[...] + _dot(o_ref[...], wo_ref[...])
    hn = _rmsnorm(x1, g_ref[...]).astype(BF16)
    dff = wd_ref.shape[0]
    fc = dff // FFN_CHUNKS
    acc = x1
    for ci in range(FFN_CHUNKS):
        gate = _dot(hn, wgu_ref[:, ci * fc:(ci + 1) * fc])
        up = _dot(hn, wgu_ref[:, dff + ci * fc:dff + (ci + 1) * fc])
        acc = acc + _dot((_silu(gate) * up).astype(BF16), wd_ref[ci * fc:(ci + 1) * fc, :])
    if final:
        acc = _rmsnorm(acc, gf_ref[...])
    y_ref[...] = acc


def _out_ffn(x2d, o2d, w_o, g_ffn, wgu, wd, g_final, final, cast_jobs=()):
    m, d = x2d.shape
    dff = wd.shape[0]
    assert dff % (FFN_CHUNKS * LANES) == 0
    steps = m // FFN_ROW_TILE
    jobs = [_cast_job(stack, layer, steps) for stack, layer in cast_jobs]
    row = lambda: pl.BlockSpec((FFN_ROW_TILE, d), lambda i: (i, 0))
    return pl.pallas_call(
        functools.partial(_out_ffn_kernel, final=final, n_cast=len(jobs)),
        grid=(steps,),
        in_specs=[row(), row(), _resident((d, d)), _resident((1, d)), _resident((d, 2 * dff)),
                  _resident((dff, d)), _resident((1, d))] + [j[0] for j in jobs],
        out_specs=[row()] + [j[1] for j in jobs],
        out_shape=[jax.ShapeDtypeStruct((m, d), F32)] + [j[2] for j in jobs],
        compiler_params=pltpu.CompilerParams(dimension_semantics=("arbitrary",),
                                             vmem_limit_bytes=57 * MIB),
        name="out_proj_ffn",
    )(x2d, o2d, w_o.astype(BF16), g_ffn.reshape(1, d), wgu, wd, g_final.reshape(1, d),
      *[stack for stack, _ in cast_jobs])


AUG_QSIDE_C = FOX_HEAD_DIM
AUG_KSIDE_C = FOX_HEAD_DIM + 3
AUG_PARTS = 3


def _split3(x):
    hi = x.astype(BF16)
    r1 = x - hi.astype(F32)
    mid = r1.astype(BF16)
    lo = (r1 - mid.astype(F32)).astype(BF16)
    return hi, mid, lo


def _kv_kernel(x_ref, g_ref, wk_ref, wvt_ref, wft_ref, bf_ref, place_ref,
               k_ref, vt_ref, c_ref, carry_ref):
    @pl.when(pl.program_id(1) == 0)
    def _():
        carry_ref[...] = jnp.zeros_like(carry_ref)

    hn = _rmsnorm(x_ref[...], g_ref[...]).astype(BF16)
    vt_ref[...] = lax.dot_general(wvt_ref[...], hn, _CONTRACT_LAST,
                                  preferred_element_type=F32).astype(BF16)
    f_t = lax.dot_general(wft_ref[...], hn, _CONTRACT_LAST, preferred_element_type=F32)
    log_f = _log_sigmoid(f_t + bf_ref[...])
    n = log_f.shape[1]
    upper = (lax.broadcasted_iota(jnp.int32, (n, n), 0)
             <= lax.broadcasted_iota(jnp.int32, (n, n), 1)).astype(F32)
    c = jnp.dot(log_f, upper, precision=HIGHEST, preferred_element_type=F32) + carry_ref[:, 0:1]
    carry_ref[...] = jnp.broadcast_to(c[:, n - 1:n], carry_ref.shape)
    c2 = c * LOG2E
    c_ref[...] = c2
    hi, mid, lo = _split3(-c2)
    parts = jnp.concatenate([hi, mid, lo, jnp.ones_like(hi)], axis=0)
    bias_lanes = lax.dot_general(parts, place_ref[...], _CONTRACT_FIRST,
                                 preferred_element_type=F32)
    k = _dot(hn, wk_ref[...])
    hd = FOX_HEAD_DIM
    low = lax.broadcasted_iota(jnp.int32, (n, LANES), 1) < hd
    for pair in range(k.shape[1] // LANES):
        kp = k[:, pair * LANES:(pair + 1) * LANES]
        for odd, src in enumerate((kp, pltpu.roll(kp, hd, axis=1))):
            cols = slice((2 * pair + odd) * LANES, (2 * pair + odd + 1) * LANES)
            k_ref[:, cols] = jnp.where(low, src, bias_lanes[:, cols]).astype(BF16)


def _kv_placement(heads):
    place = jnp.zeros((4 * heads, heads * LANES), F32)
    h = jnp.arange(heads)
    for part in range(AUG_PARTS):
        place = place.at[part * heads + h, h * LANES + AUG_KSIDE_C + part].set(1.0)
        place = place.at[AUG_PARTS * heads, h * LANES + AUG_QSIDE_C + part].set(1.0)
    return place.astype(BF16)


def _shared_kv(x3d, g, w_kv, b_f):
    batch, seq, d = x3d.shape
    heads = b_f.shape[0]
    wk = w_kv[:, :d].astype(BF16)
    wvt = w_kv[:, d:2 * d].T.astype(BF16)
    wft = w_kv[:, 2 * d:].T.astype(BF16)
    kw = heads * LANES
    return pl.pallas_call(
        _kv_kernel,
        grid=(batch, seq // KV_TILE),
        in_specs=[pl.BlockSpec((None, KV_TILE, d), lambda b, t: (b, t, 0)),
                  _resident((1, d)), _resident((d, d)), _resident((d, d)),
                  _resident((heads, d)), _resident((heads, 1)), _resident((4 * heads, kw))],
        out_specs=[pl.BlockSpec((None, KV_TILE, kw), lambda b, t: (b, t, 0)),
                   pl.BlockSpec((None, d, KV_TILE), lambda b, t: (b, 0, t)),
                   pl.BlockSpec((None, heads, KV_TILE), lambda b, t: (b, 0, t))],
        out_shape=[jax.ShapeDtypeStruct((batch, seq, kw), BF16),
                   jax.ShapeDtypeStruct((batch, d, seq), BF16),
                   jax.ShapeDtypeStruct((batch, heads, seq), F32)],
        scratch_shapes=[pltpu.VMEM((heads, LANES), F32)],
        compiler_params=pltpu.CompilerParams(dimension_semantics=("parallel", "arbitrary"),
                                             vmem_limit_bytes=40 * MIB),
        name="shared_kv",
    )(x3d, g.reshape(1, d), wk, wvt, wft, b_f.reshape(heads, 1), _kv_placement(heads))


def _q_kernel(x_ref, g_ref, wqt_ref, c_ref, qt_ref):
    hd = FOX_HEAD_DIM
    hn = _rmsnorm(x_ref[...], g_ref[...]).astype(BF16)
    qt = lax.dot_general(wqt_ref[...], hn, _CONTRACT_LAST, preferred_element_type=F32)
    qt = (qt * (hd ** -0.5 * LOG2E)).astype(BF16)
    hi, mid, lo = (p.astype(F32) for p in _split3(c_ref[...]))
    n = hn.shape[0]
    r = lax.broadcasted_iota(jnp.int32, (BF16_SUBLANES, n), 0)
    for h in range(c_ref.shape[0]):
        base = h * LANES
        bias = jnp.where(r == 0, hi[h:h + 1], jnp.where(r == 1, mid[h:h + 1], jnp.where(
            r == 2, lo[h:h + 1], jnp.where(r < 2 * AUG_PARTS, 1.0, 0.0))))
        qt_ref[base:base + hd, :] = qt[h * hd:(h + 1) * hd, :]
        qt_ref[base + hd:base + hd + BF16_SUBLANES, :] = bias.astype(BF16)
        qt_ref[base + hd + BF16_SUBLANES:base + LANES, :] = jnp.zeros(
            (LANES - hd - BF16_SUBLANES, n), BF16)


def _q_proj(x3d, g, w_q, c2):
    batch, seq, d = x3d.shape
    heads = c2.shape[1]
    return pl.pallas_call(
        _q_kernel,
        grid=(batch, seq // ROW_TILE),
        in_specs=[pl.BlockSpec((None, ROW_TILE, d), lambda b, t: (b, t, 0)),
                  _resident((1, d)), _resident((d, d)),
                  pl.BlockSpec((None, heads, ROW_TILE), lambda b, t: (b, 0, t))],
        out_specs=pl.BlockSpec((None, heads * LANES, ROW_TILE), lambda b, t: (b, 0, t)),
        out_shape=jax.ShapeDtypeStruct((batch, heads * LANES, seq), BF16),
        compiler_params=pltpu.CompilerParams(dimension_semantics=("parallel", "parallel"),
                                             vmem_limit_bytes=40 * MIB),
        name="fox_q_proj",
    )(x3d, g.reshape(1, d), w_q.T.astype(BF16), c2)


def _fox_kernel(qt_ref, k_ref, vt_ref, o_ref, s_ref, m_ref, acc_ref, *, n_q):
    t = FOX_TILE
    hd = FOX_HEAD_DIM
    heads = range(FOX_HEADS_PER_STEP)
    causal = (lax.broadcasted_iota(jnp.int32, (t, t), 0)
              <= lax.broadcasted_iota(jnp.int32, (t, t), 1))
    ones = jnp.ones((BF16_SUBLANES, t), BF16)

    def logits(qi, j, slot):
        keys = pl.ds(j * t, t)
        queries = pl.ds(qi * t, t)
        for h in heads:
            s_ref[slot, h] = _dot(k_ref[keys, h * LANES:(h + 1) * LANES],
                                  qt_ref[h * LANES:(h + 1) * LANES, queries])

    def update(qi, j, slot, diagonal):
        keys = pl.ds(j * t, t)

        def tile(h):
            s = s_ref[slot, h]
            return jnp.where(causal, s, NEG_BIG) if diagonal else s

        if diagonal:
            m_new = [jnp.max(tile(h), axis=0, keepdims=True) for h in heads]
        else:
            m_old = [m_ref[qi, h] for h in heads]
            m_new = [jnp.maximum(m_old[h], jnp.max(tile(h), axis=0, keepdims=True)) for h in heads]
            alpha = [jnp.exp2(m_old[h] - m_new[h]) for h in heads]
        p = [jnp.exp2(tile(h) - m_new[h]).astype(BF16) for h in heads]
        pv = [_dot(jnp.concatenate([vt_ref[h * hd:(h + 1) * hd, keys], ones], axis=0), p[h])
              for h in heads]
        for h in heads:
            m_ref[qi, h] = m_new[h]
            acc_ref[qi, h] = pv[h] if diagonal else alpha[h] * acc_ref[qi, h] + pv[h]

    ns, ahead = FOX_SLOTS, FOX_LOOKAHEAD
    tiles = [(qi, qi) for qi in range(n_q)]
    tiles += [(qi, j) for j in range(n_q) for qi in range(j + 1, n_q)]
    for w in range(ahead):
        logits(*tiles[w], w)
    for k, (qi, j) in enumerate(tiles):
        if k + ahead < len(tiles):
            logits(*tiles[k + ahead], (k + ahead) % ns)
        update(qi, j, k % ns, qi == j)

    for qi in range(n_q):
        o_t = jnp.concatenate([acc_ref[qi, h, :hd] / acc_ref[qi, h, hd:hd + 1] for h in heads],
                              axis=0)
        o_ref[pl.ds(qi * t, t), :] = o_t.T.astype(BF16)


def _fox_attention(qt, k_aug, vt, batch, seq, d):
    nh = FOX_HEADS_PER_STEP
    groups = d // (nh * FOX_HEAD_DIM)
    t = FOX_TILE
    out = pl.pallas_call(
        functools.partial(_fox_kernel, n_q=seq // t),
        grid=(batch, groups),
        in_specs=[pl.BlockSpec((None, nh * LANES, seq), lambda b, p: (b, p, 0)),
                  pl.BlockSpec((None, seq, nh * LANES), lambda b, p: (b, 0, p)),
                  pl.BlockSpec((None, nh * FOX_HEAD_DIM, seq), lambda b, p: (b, p, 0))],
        out_specs=pl.BlockSpec((None, seq, nh * FOX_HEAD_DIM), lambda b, p: (b, 0, p)),
        out_shape=jax.ShapeDtypeStruct((batch, seq, d), BF16),
        scratch_shapes=[pltpu.VMEM((FOX_SLOTS, nh, t, t), F32),
                        pltpu.VMEM((seq // t, nh, 1, t), F32),
                        pltpu.VMEM((seq // t, nh, FOX_HEAD_DIM + BF16_SUBLANES, t), F32)],
        compiler_params=pltpu.CompilerParams(
            dimension_semantics=("parallel", "parallel"),
            vmem_limit_bytes=40 * MIB),
        name="fox_attention",
    )(qt, k_aug, vt)
    return out.reshape(batch * seq, d)


def kernel(x, attn_norm, ffn_norm, gla_w_in, gla_w_alpha_up, gla_b_alpha, gla_g_head, gla_w_out,
           kv_norm, w_kv, b_f, fox_w_q, fox_w_o, ffn_w_gu, ffn_w_down, final_norm):
    batch, seq, d = x.shape
    depth = attn_norm.shape[0]
    n_gla = gla_w_in.shape[0]
    assert (batch * seq) % ROW_TILE == 0 and seq % GLA_TIME_TILE == 0 and seq % ROW_TILE == 0
    assert seq % KV_TILE == 0 and seq % FOX_TILE == 0
    assert (FOX_HEADS_PER_STEP * FOX_HEAD_DIM) % LANES == 0
    assert d // b_f.shape[0] == FOX_HEAD_DIM

    h = x.reshape(batch * seq, d)
    k_aug = v_t = c2 = None
    ffn_w = None
    for layer in range(depth):
        if layer < n_gla:
            jobs = [(ffn_w_gu, 0), (ffn_w_down, 0)] if layer == 0 else []
            q, k, v, r, la, *cast = _gla_in_proj(h, attn_norm[layer], gla_w_in[layer],
                                                 gla_w_alpha_up[layer], gla_b_alpha[layer], jobs)
            ffn_w = cast or ffn_w
            o = _gla_recurrence(q, k, v, r, la, gla_g_head[layer], batch, seq)
            w_o = gla_w_out[layer]
        else:
            h3 = h.reshape(batch, seq, d)
            if layer == n_gla:
                k_aug, v_t, c2 = _shared_kv(h3, kv_norm, w_kv, b_f)
            j = layer - n_gla
            q_t = _q_proj(h3, attn_norm[layer], fox_w_q[j], c2)
            o = _fox_attention(q_t, k_aug, v_t, batch, seq, d)
            w_o = fox_w_o[j]
        if ffn_w is None:
            ffn_w = [ffn_w_gu[layer].astype(BF16), ffn_w_down[layer].astype(BF16)]
        jobs = [(ffn_w_gu, layer + 1), (ffn_w_down, layer + 1)] if layer + 1 < depth else []
        h, *cast = _out_ffn(h, o, w_o, ffn_norm[layer], ffn_w[0], ffn_w[1], final_norm,
                            final=(layer == depth - 1), cast_jobs=jobs)
        ffn_w = cast or None
    return h.reshape(batch, seq, d)
```

```python
import functools
import math

import jax
import jax.numpy as jnp
from jax import lax
from jax.experimental import pallas as pl
from jax.experimental.pallas import tpu as pltpu

F32 = jnp.float32
BF16 = jnp.bfloat16
HIGHEST = lax.Precision.HIGHEST

EPS = 1e-6
GLA_HEADS = 4
GLA_RANK = 16
GLA_TAU = 16.0
GLA_CHUNK = 64
FOX_HEAD_DIM = 64
LOG2E = math.log2(math.e)

LANES = 128
MXU_COLS = 256
BF16_SUBLANES = 16
MIB = 1024 * 1024

ROW_TILE = 1024
FFN_ROW_TILE = 1024
GLA_TIME_TILE = 1024
GLA_PROJ_SLICE = 256
KV_TILE = 1024
FOX_TILE = 256
FOX_HEADS_PER_STEP = 2
FOX_GROUPS_PER_STEP = 2
FOX_SLOTS = 4
FOX_LOOKAHEAD = 2
FFN_CHUNKS = 11
NEG_BIG = -1e30

_CONTRACT_LAST = (((1,), (1,)), ((), ()))
_CONTRACT_FIRST = (((0,), (0,)), ((), ()))


def _rmsnorm(x, g):
    return x * lax.rsqrt(jnp.mean(x * x, axis=-1, keepdims=True) + EPS) * g


def _log_sigmoid(z):
    return jnp.minimum(z, 0.0) - jnp.log(1.0 + jnp.exp(-jnp.abs(z)))


def _silu(x):
    return x * jax.nn.sigmoid(x)


def _dot(a, b):
    return jnp.dot(a, b, preferred_element_type=F32)


def _resident(shape):
    return pl.BlockSpec(shape, lambda *_: (0,) * len(shape), pipeline_mode=pl.Buffered(1))


def _cast_job(stack, layer, steps):
    _, rows, cols = stack.shape
    nb = max(n for n in range(1, steps + 1)
             if rows % n == 0 and (rows // n) % BF16_SUBLANES == 0)
    br = rows // nb
    in_spec = pl.BlockSpec((None, br, cols), lambda i: (layer, jnp.minimum(i, nb - 1), 0))
    out_spec = pl.BlockSpec((br, cols), lambda i: (jnp.minimum(i, nb - 1), 0))
    return in_spec, out_spec, jax.ShapeDtypeStruct((rows, cols), BF16)


def _run_cast_jobs(src_refs, dst_refs):
    for src, dst in zip(src_refs, dst_refs):
        dst[...] = src[...].astype(BF16)


def _gla_mixer_kernel(x_ref, g_ref, wq_ref, wk_ref, wv_ref, wr_ref, wa_ref, wup_ref, ba_ref, gh_ref,
                      *refs, n_cast, dk, dv, tiles_per_seq):
    cast_src, o_ref, cast_dst = refs[:n_cast], refs[n_cast], refs[n_cast + 1:n_cast + 1 + n_cast]
    scratch = refs[n_cast + 1 + n_cast:]
    st_ref = scratch[-1]
    slice_refs = [scratch[6 * s:6 * s + 6] for s in range((len(scratch) - 1) // 6)]
    _run_cast_jobs(cast_src, cast_dst)
    c = GLA_CHUNK
    tile = x_ref.shape[0]

    @pl.when(pl.program_id(0) % tiles_per_seq == 0)
    def _():
        st_ref[...] = jnp.zeros_like(st_ref)

    per_slice = GLA_PROJ_SLICE // c

    def projection_pieces(s):
        rows = pl.ds(s * GLA_PROJ_SLICE, GLA_PROJ_SLICE)
        hn_s, q_s, k_s, v_s, r_s, la_s = slice_refs[s]

        def norm_and_gate():
            hn = _rmsnorm(x_ref[rows, :], g_ref[...]).astype(BF16)
            hn_s[...] = hn
            a_low = _dot(hn, wa_ref[...])
            z = _dot(a_low.astype(BF16), wup_ref[...]) + ba_ref[...]
            la_s[...] = _log_sigmoid(z) * (1.0 / GLA_TAU)

        def column_block(dst, w_ref, j):
            cols = slice(j * MXU_COLS, (j + 1) * MXU_COLS)

            def run():
                dst[:, cols] = _dot(hn_s[...], w_ref[:, cols]).astype(BF16)
            return run

        pieces = [norm_and_gate]
        for dst, w_ref in ((q_s, wq_ref), (k_s, wk_ref), (v_s, wv_ref), (r_s, wr_ref)):
            pieces += [column_block(dst, w_ref, j) for j in range(w_ref.shape[1] // MXU_COLS)]
        return pieces

    causal = (lax.broadcasted_iota(jnp.int32, (c, c), 0)
              >= lax.broadcasted_iota(jnp.int32, (c, c), 1))
    row = lax.broadcasted_iota(jnp.int32, (c, GLA_HEADS * dk), 0)
    g = gh_ref[...]
    scale = dk ** -0.5
    heads = range(GLA_HEADS)
    kcols = [slice(h * dk, (h + 1) * dk) for h in heads]
    vcols = [slice(h * dv, (h + 1) * dv) for h in heads]

    def chunk(n):
        _, q_s, k_s, v_s, r_s, la_s = slice_refs[n // per_slice]
        rows = pl.ds((n % per_slice) * c, c)
        out_rows = pl.ds(n * c, c)
        b = la_s[rows, :]
        shift = 1
        while shift < c:
            b = b + jnp.where(row >= shift, pltpu.roll(b, shift, axis=0), 0.0)
            shift *= 2
        b_last = b[c - 1:c, :]
        yield
        qf = q_s[rows, :].astype(F32) * scale
        kf = k_s[rows, :].astype(F32)
        q_dec = (qf * jnp.exp(b)).astype(BF16)
        k_dec = (kf * jnp.exp(-b)).astype(BF16)
        k_out = (kf * jnp.exp(b_last - b)).astype(BF16)
        decay = jnp.exp(b_last)
        yield
        v = [v_s[rows, vcols[h]] for h in heads]
        a = [lax.dot_general(q_dec[:, kcols[h]], k_dec[:, kcols[h]], _CONTRACT_LAST,
                             preferred_element_type=F32) for h in heads]
        a = [jnp.where(causal, a[h], 0.0).astype(BF16) for h in heads]
        st = [st_ref[h] for h in heads]
        o = [_dot(a[h], v[h]) + lax.dot_general(q_dec[:, kcols[h]], st[h].astype(BF16),
                                                _CONTRACT_LAST, preferred_element_type=F32)
             for h in heads]
        yield
        kv = [lax.dot_general(v[h], k_out[:, kcols[h]], _CONTRACT_FIRST,
                              preferred_element_type=F32) for h in heads]
        for h in heads:
            st_ref[h] = st[h] * decay[:, kcols[h]] + kv[h]
            gate = _silu(r_s[rows, vcols[h]].astype(F32))
            o_ref[out_rows, vcols[h]] = (_rmsnorm(o[h], g) * gate).astype(BF16)

    n_slices = tile // GLA_PROJ_SLICE
    for piece in projection_pieces(0):
        piece()
    for s in range(n_slices):
        ahead = projection_pieces(s + 1) if s + 1 < n_slices else []
        for n in range(s * per_slice, (s + 1) * per_slice):
            stages = chunk(n)
            running = True
            while running:
                if ahead:
                    ahead.pop(0)()
                running = next(stages, "done") != "done"
        assert not ahead


def _gla_mixer(x2d, g, w_in, w_up, b_alpha, g_head, seq, cast_jobs=()):
    m, d = x2d.shape
    tile, sl = GLA_TIME_TILE, GLA_PROJ_SLICE
    steps = m // tile
    jobs = [_cast_job(stack, layer, steps) for stack, layer in cast_jobs]
    qk = w_up.shape[1]
    vdim = (w_in.shape[1] - 2 * qk - GLA_RANK) // 2
    dk, dv = qk // GLA_HEADS, vdim // GLA_HEADS
    wq = w_in[:, :qk].astype(BF16)
    wk = w_in[:, qk:2 * qk].astype(BF16)
    wv = w_in[:, 2 * qk:2 * qk + vdim].astype(BF16)
    wr = w_in[:, 2 * qk + vdim:2 * qk + 2 * vdim].astype(BF16)
    wa = jnp.pad(w_in[:, 2 * qk + 2 * vdim:], ((0, 0), (0, LANES - GLA_RANK))).astype(BF16)
    wup = jnp.pad(w_up, ((0, LANES - GLA_RANK), (0, 0))).astype(BF16)
    row = lambda n: pl.BlockSpec((tile, n), lambda i: (i, 0))
    return pl.pallas_call(
        functools.partial(_gla_mixer_kernel, n_cast=len(jobs), dk=dk, dv=dv,
                          tiles_per_seq=seq // tile),
        grid=(steps,),
        in_specs=[row(d), _resident((1, d)), _resident((d, qk)), _resident((d, qk)),
                  _resident((d, vdim)), _resident((d, vdim)), _resident((d, LANES)),
                  _resident((LANES, qk)), _resident((1, qk)), _resident((1, dv))]
        + [j[0] for j in jobs],
        out_specs=[row(vdim)] + [j[1] for j in jobs],
        out_shape=[jax.ShapeDtypeStruct((m, vdim), BF16)] + [j[2] for j in jobs],
        scratch_shapes=[pltpu.VMEM((sl, w), t) for _ in range(tile // sl) for w, t in (
            (d, BF16), (qk, BF16), (qk, BF16), (vdim, BF16), (vdim, BF16), (qk, F32))]
        + [pltpu.VMEM((GLA_HEADS, dv, dk), F32)],
        compiler_params=pltpu.CompilerParams(dimension_semantics=("arbitrary",),
                                             vmem_limit_bytes=48 * MIB),
        name="gla_mixer",
    )(x2d, g.reshape(1, d), wq, wk, wv, wr, wa, wup, b_alpha.reshape(1, qk),
      g_head.reshape(1, dv), *[stack for stack, _ in cast_jobs])


def _out_ffn_kernel(x_ref, o_ref, wo_ref, g_ref, wgu_ref, wd_ref, gf_ref, *refs, final, n_cast):
    cast_src, y_ref, cast_dst = refs[:n_cast], refs[n_cast], refs[n_cast + 1:]
    _run_cast_jobs(cast_src, cast_dst)
    x1 = x_ref[...] + _dot(o_ref[...], wo_ref[...])
    hn = _rmsnorm(x1, g_ref[...]).astype(BF16)
    dff = wd_ref.shape[0]
    fc = dff // FFN_CHUNKS
    acc = x1
    for ci in range(FFN_CHUNKS):
        gate = _dot(hn, wgu_ref[:, ci * fc:(ci + 1) * fc])
        up = _dot(hn, wgu_ref[:, dff + ci * fc:dff + (ci + 1) * fc])
        acc = acc + _dot((_silu(gate) * up).astype(BF16), wd_ref[ci * fc:(ci + 1) * fc, :])
    if final:
        acc = _rmsnorm(acc, gf_ref[...])
    y_ref[...] = acc


def _out_ffn(x2d, o2d, w_o, g_ffn, wgu, wd, g_final, final, cast_jobs=()):
    m, d = x2d.shape
    dff = wd.shape[0]
    assert dff % (FFN_CHUNKS * LANES) == 0
    steps = m // FFN_ROW_TILE
    jobs = [_cast_job(stack, layer, steps) for stack, layer in cast_jobs]
    row = lambda: pl.BlockSpec((FFN_ROW_TILE, d), lambda i: (i, 0))
    return pl.pallas_call(
        functools.partial(_out_ffn_kernel, final=final, n_cast=len(jobs)),
        grid=(steps,),
        in_specs=[row(), row(), _resident((d, d)), _resident((1, d)), _resident((d, 2 * dff)),
                  _resident((dff, d)), _resident((1, d))] + [j[0] for j in jobs],
        out_specs=[row()] + [j[1] for j in jobs],
        out_shape=[jax.ShapeDtypeStruct((m, d), F32)] + [j[2] for j in jobs],
        compiler_params=pltpu.CompilerParams(dimension_semantics=("arbitrary",),
                                             vmem_limit_bytes=57 * MIB),
        name="out_proj_ffn",
    )(x2d, o2d, w_o.astype(BF16), g_ffn.reshape(1, d), wgu, wd, g_final.reshape(1, d),
      *[stack for stack, _ in cast_jobs])


AUG_QSIDE_C = FOX_HEAD_DIM
AUG_KSIDE_C = FOX_HEAD_DIM + 3
AUG_PARTS = 3


def _split3(x):
    hi = x.astype(BF16)
    r1 = x - hi.astype(F32)
    mid = r1.astype(BF16)
    lo = (r1 - mid.astype(F32)).astype(BF16)
    return hi, mid, lo


def _kv_kernel(x_ref, g_ref, wk_ref, wvt_ref, wft_ref, bf_ref, place_ref,
               k_ref, vt_ref, c_ref, carry_ref):
    @pl.when(pl.program_id(1) == 0)
    def _():
        carry_ref[...] = jnp.zeros_like(carry_ref)

    hn = _rmsnorm(x_ref[...], g_ref[...]).astype(BF16)
    k = _dot(hn, wk_ref[...])
    f_t = lax.dot_general(wft_ref[...], hn, _CONTRACT_LAST, preferred_element_type=F32)
    log_f = _log_sigmoid(f_t + bf_ref[...])
    n = log_f.shape[1]
    upper = (lax.broadcasted_iota(jnp.int32, (n, n), 0)
             <= lax.broadcasted_iota(jnp.int32, (n, n), 1)).astype(F32)
    c = jnp.dot(log_f, upper, precision=HIGHEST, preferred_element_type=F32) + carry_ref[:, 0:1]
    carry_ref[...] = jnp.broadcast_to(c[:, n - 1:n], carry_ref.shape)
    c2 = c * LOG2E
    c_ref[...] = c2
    hi, mid, lo = _split3(-c2)
    parts = jnp.concatenate([hi, mid, lo, jnp.ones_like(hi)], axis=0)
    bias_lanes = lax.dot_general(parts, place_ref[...], _CONTRACT_FIRST,
                                 preferred_element_type=F32)
    vt_ref[...] = lax.dot_general(wvt_ref[...], hn, _CONTRACT_LAST,
                                  preferred_element_type=F32).astype(BF16)
    hd = FOX_HEAD_DIM
    low = lax.broadcasted_iota(jnp.int32, (n, LANES), 1) < hd
    for pair in range(k.shape[1] // LANES):
        kp = k[:, pair * LANES:(pair + 1) * LANES]
        for odd, src in enumerate((kp, pltpu.roll(kp, hd, axis=1))):
            cols = slice((2 * pair + odd) * LANES, (2 * pair + odd + 1) * LANES)
            k_ref[:, cols] = jnp.where(low, src, bias_lanes[:, cols]).astype(BF16)


def _kv_placement(heads):
    place = jnp.zeros((4 * heads, heads * LANES), F32)
    h = jnp.arange(heads)
    for part in range(AUG_PARTS):
        place = place.at[part * heads + h, h * LANES + AUG_KSIDE_C + part].set(1.0)
        place = place.at[AUG_PARTS * heads, h * LANES + AUG_QSIDE_C + part].set(1.0)
    return place.astype(BF16)


def _shared_kv(x3d, g, w_kv, b_f):
    batch, seq, d = x3d.shape
    heads = b_f.shape[0]
    wk = w_kv[:, :d].astype(BF16)
    wvt = w_kv[:, d:2 * d].T.astype(BF16)
    wft = w_kv[:, 2 * d:].T.astype(BF16)
    kw = heads * LANES
    return pl.pallas_call(
        _kv_kernel,
        grid=(batch, seq // KV_TILE),
        in_specs=[pl.BlockSpec((None, KV_TILE, d), lambda b, t: (b, t, 0)),
                  _resident((1, d)), _resident((d, d)), _resident((d, d)),
                  _resident((heads, d)), _resident((heads, 1)), _resident((4 * heads, kw))],
        out_specs=[pl.BlockSpec((None, KV_TILE, kw), lambda b, t: (b, t, 0)),
                   pl.BlockSpec((None, d, KV_TILE), lambda b, t: (b, 0, t)),
                   pl.BlockSpec((None, heads, KV_TILE), lambda b, t: (b, 0, t))],
        out_shape=[jax.ShapeDtypeStruct((batch, seq, kw), BF16),
                   jax.ShapeDtypeStruct((batch, d, seq), BF16),
                   jax.ShapeDtypeStruct((batch, heads, seq), F32)],
        scratch_shapes=[pltpu.VMEM((heads, LANES), F32)],
        compiler_params=pltpu.CompilerParams(dimension_semantics=("parallel", "arbitrary"),
                                             vmem_limit_bytes=40 * MIB),
        name="shared_kv",
    )(x3d, g.reshape(1, d), wk, wvt, wft, b_f.reshape(heads, 1), _kv_placement(heads))


def _q_kernel(x_ref, g_ref, wqt_ref, c_ref, qt_ref):
    hd = FOX_HEAD_DIM
    hn = _rmsnorm(x_ref[...], g_ref[...]).astype(BF16)
    qt = lax.dot_general(wqt_ref[...], hn, _CONTRACT_LAST, preferred_element_type=F32)
    qt = (qt * (hd ** -0.5 * LOG2E)).astype(BF16)
    hi, mid, lo = (p.astype(F32) for p in _split3(c_ref[...]))
    n = hn.shape[0]
    r = lax.broadcasted_iota(jnp.int32, (BF16_SUBLANES, n), 0)
    for h in range(c_ref.shape[0]):
        base = h * LANES
        bias = jnp.where(r == 0, hi[h:h + 1], jnp.where(r == 1, mid[h:h + 1], jnp.where(
            r == 2, lo[h:h + 1], jnp.where(r < 2 * AUG_PARTS, 1.0, 0.0))))
        qt_ref[base:base + hd, :] = qt[h * hd:(h + 1) * hd, :]
        qt_ref[base + hd:base + hd + BF16_SUBLANES, :] = bias.astype(BF16)
        qt_ref[base + hd + BF16_SUBLANES:base + LANES, :] = jnp.zeros(
            (LANES - hd - BF16_SUBLANES, n), BF16)


def _q_proj(x3d, g, w_q, c2):
    batch, seq, d = x3d.shape
    heads = c2.shape[1]
    return pl.pallas_call(
        _q_kernel,
        grid=(batch, seq // ROW_TILE),
        in_specs=[pl.BlockSpec((None, ROW_TILE, d), lambda b, t: (b, t, 0)),
                  _resident((1, d)), _resident((d, d)),
                  pl.BlockSpec((None, heads, ROW_TILE), lambda b, t: (b, 0, t))],
        out_specs=pl.BlockSpec((None, heads * LANES, ROW_TILE), lambda b, t: (b, 0, t)),
        out_shape=jax.ShapeDtypeStruct((batch, heads * LANES, seq), BF16),
        compiler_params=pltpu.CompilerParams(dimension_semantics=("parallel", "parallel"),
                                             vmem_limit_bytes=40 * MIB),
        name="fox_q_proj",
    )(x3d, g.reshape(1, d), w_q.T.astype(BF16), c2)


def _fox_kernel(qt_ref, k_ref, vt_ref, o_ref, s_ref, m_ref, acc_ref, *, n_q):
    t = FOX_TILE
    causal = (lax.broadcasted_iota(jnp.int32, (t, t), 0)
              <= lax.broadcasted_iota(jnp.int32, (t, t), 1))
    ones = jnp.ones((BF16_SUBLANES, t), BF16)
    for g0 in range(0, FOX_GROUPS_PER_STEP * FOX_HEADS_PER_STEP, FOX_HEADS_PER_STEP):
        _fox_group(qt_ref, k_ref, vt_ref, o_ref, s_ref, m_ref, acc_ref, g0, n_q, causal, ones)


def _fox_group(qt_ref, k_ref, vt_ref, o_ref, s_ref, m_ref, acc_ref, g0, n_q, causal, ones):
    t = FOX_TILE
    hd = FOX_HEAD_DIM
    heads = range(FOX_HEADS_PER_STEP)

    def logits(qi, j, slot):
        keys = pl.ds(j * t, t)
        queries = pl.ds(qi * t, t)
        for h in heads:
            s_ref[slot, h] = _dot(k_ref[keys, (g0 + h) * LANES:(g0 + h + 1) * LANES],
                                  qt_ref[(g0 + h) * LANES:(g0 + h + 1) * LANES, queries])

    def update(qi, j, slot, diagonal):
        keys = pl.ds(j * t, t)

        def tile(h):
            s = s_ref[slot, h]
            return jnp.where(causal, s, NEG_BIG) if diagonal else s

        if diagonal:
            m_new = [jnp.max(tile(h), axis=0, keepdims=True) for h in heads]
        else:
            m_old = [m_ref[qi, h] for h in heads]
            m_new = [jnp.maximum(m_old[h], jnp.max(tile(h), axis=0, keepdims=True)) for h in heads]
            alpha = [jnp.exp2(m_old[h] - m_new[h]) for h in heads]
        p = [jnp.exp2(tile(h) - m_new[h]).astype(BF16) for h in heads]
        pv = [_dot(jnp.concatenate([vt_ref[(g0 + h) * hd:(g0 + h + 1) * hd, keys], ones], axis=0),
                   p[h]) for h in heads]
        for h in heads:
            m_ref[qi, h] = m_new[h]
            acc_ref[qi, h] = pv[h] if diagonal else alpha[h] * acc_ref[qi, h] + pv[h]

    ns, ahead = FOX_SLOTS, FOX_LOOKAHEAD
    tiles = [(qi, qi) for qi in range(n_q)]
    tiles += [(qi, j) for j in range(n_q) for qi in range(j + 1, n_q)]
    for w in range(ahead):
        logits(*tiles[w], w)
    for k, (qi, j) in enumerate(tiles):
        if k + ahead < len(tiles):
            logits(*tiles[k + ahead], (k + ahead) % ns)
        update(qi, j, k % ns, qi == j)

    for qi in range(n_q):
        o_t = jnp.concatenate([acc_ref[qi, h, :hd] / acc_ref[qi, h, hd:hd + 1] for h in heads],
                              axis=0)
        o_ref[pl.ds(qi * t, t), g0 * hd:(g0 + FOX_HEADS_PER_STEP) * hd] = o_t.T.astype(BF16)


def _fox_attention(qt, k_aug, vt, batch, seq, d):
    nh = FOX_HEADS_PER_STEP
    nb = nh * FOX_GROUPS_PER_STEP
    groups = d // (nb * FOX_HEAD_DIM)
    t = FOX_TILE
    out = pl.pallas_call(
        functools.partial(_fox_kernel, n_q=seq // t),
        grid=(batch, groups),
        in_specs=[pl.BlockSpec((None, nb * LANES, seq), lambda b, p: (b, p, 0)),
                  pl.BlockSpec((None, seq, nb * LANES), lambda b, p: (b, 0, p)),
                  pl.BlockSpec((None, nb * FOX_HEAD_DIM, seq), lambda b, p: (b, p, 0))],
        out_specs=pl.BlockSpec((None, seq, nb * FOX_HEAD_DIM), lambda b, p: (b, 0, p)),
        out_shape=jax.ShapeDtypeStruct((batch, seq, d), BF16),
        scratch_shapes=[pltpu.VMEM((FOX_SLOTS, nh, t, t), F32),
                        pltpu.VMEM((seq // t, nh, 1, t), F32),
                        pltpu.VMEM((seq // t, nh, FOX_HEAD_DIM + BF16_SUBLANES, t), F32)],
        compiler_params=pltpu.CompilerParams(
            dimension_semantics=("parallel", "parallel"),
            vmem_limit_bytes=40 * MIB),
        name="fox_attention",
    )(qt, k_aug, vt)
    return out.reshape(batch * seq, d)


def kernel(x, attn_norm, ffn_norm, gla_w_in, gla_w_alpha_up, gla_b_alpha, gla_g_head, gla_w_out,
           kv_norm, w_kv, b_f, fox_w_q, fox_w_o, ffn_w_gu, ffn_w_down, final_norm):
    batch, seq, d = x.shape
    depth = attn_norm.shape[0]
    n_gla = gla_w_in.shape[0]
    assert seq % GLA_TIME_TILE == 0 and seq % ROW_TILE == 0
    assert GLA_TIME_TILE % GLA_PROJ_SLICE == 0 and GLA_PROJ_SLICE % GLA_CHUNK == 0
    assert seq % KV_TILE == 0 and seq % FOX_TILE == 0
    assert (FOX_HEADS_PER_STEP * FOX_HEAD_DIM) % LANES == 0
    assert d // b_f.shape[0] == FOX_HEAD_DIM

    h = x.reshape(batch * seq, d)
    k_aug = v_t = c2 = None
    ffn_w = None
    for layer in range(depth):
        if layer < n_gla:
            jobs = [(ffn_w_gu, 0), (ffn_w_down, 0)] if layer == 0 else []
            o, *cast = _gla_mixer(h, attn_norm[layer], gla_w_in[layer], gla_w_alpha_up[layer],
                                  gla_b_alpha[layer], gla_g_head[layer], seq, jobs)
            ffn_w = cast or ffn_w
            w_o = gla_w_out[layer]
        else:
            h3 = h.reshape(batch, seq, d)
            if layer == n_gla:
                k_aug, v_t, c2 = _shared_kv(h3, kv_norm, w_kv, b_f)
            j = layer - n_gla
            q_t = _q_proj(h3, attn_norm[layer], fox_w_q[j], c2)
            o = _fox_attention(q_t, k_aug, v_t, batch, seq, d)
            w_o = fox_w_o[j]
        if ffn_w is None:
            ffn_w = [ffn_w_gu[layer].astype(BF16), ffn_w_down[layer].astype(BF16)]
        jobs = [(ffn_w_gu, layer + 1), (ffn_w_down, layer + 1)] if layer + 1 < depth else []
        h, *cast = _out_ffn(h, o, w_o, ffn_norm[layer], ffn_w[0], ffn_w[1], final_norm,
                            final=(layer == depth - 1), cast_jobs=jobs)
        ffn_w = cast or None
    return h.reshape(batch, seq, d)
```

```python
import functools
import math

import jax
import jax.numpy as jnp
import numpy as np
from jax import lax
from jax.experimental import pallas as pl
from jax.experimental.pallas import tpu as pltpu

F32 = jnp.float32
BF16 = jnp.bfloat16
HIGHEST = lax.Precision.HIGHEST

EPS = 1e-6
GLA_HEADS = 4
GLA_RANK = 16
GLA_TAU = 16.0
GLA_CHUNK = 64
FOX_HEAD_DIM = 64
LOG2E = math.log2(math.e)

LANES = 128
MXU_COLS = 256
BF16_SUBLANES = 16
MIB = 1024 * 1024

ROW_TILE = 1024
FFN_ROW_TILE = 1024
GLA_TIME_TILE = 1024
GLA_PROJ_SLICE = 256
KV_TILE = 1024
FOX_TILE = 256
FOX_HEADS_PER_STEP = 2
FOX_GROUPS_PER_STEP = 2
FOX_SLOTS = 4
FOX_LOOKAHEAD = 2
FFN_CHUNKS = 11
NEG_BIG = -1e30

_CONTRACT_LAST = (((1,), (1,)), ((), ()))
_CONTRACT_FIRST = (((0,), (0,)), ((), ()))


def _rmsnorm(x, g):
    return x * lax.rsqrt(jnp.mean(x * x, axis=-1, keepdims=True) + EPS) * g


def _log_sigmoid(z):
    return jnp.minimum(z, 0.0) - jnp.log(1.0 + jnp.exp(-jnp.abs(z)))


def _silu(x):
    return x * jax.nn.sigmoid(x)


def _dot(a, b):
    return jnp.dot(a, b, preferred_element_type=F32)


def _resident(shape):
    return pl.BlockSpec(shape, lambda *_: (0,) * len(shape), pipeline_mode=pl.Buffered(1))


def _cast_job(stack, layer, steps):
    _, rows, cols = stack.shape
    nb = max(n for n in range(1, steps + 1)
             if rows % n == 0 and (rows // n) % BF16_SUBLANES == 0)
    br = rows // nb
    in_spec = pl.BlockSpec((None, br, cols), lambda i: (layer, jnp.minimum(i, nb - 1), 0))
    out_spec = pl.BlockSpec((br, cols), lambda i: (jnp.minimum(i, nb - 1), 0))
    return in_spec, out_spec, jax.ShapeDtypeStruct((rows, cols), BF16)


def _run_cast_jobs(src_refs, dst_refs):
    for src, dst in zip(src_refs, dst_refs):
        dst[...] = src[...].astype(BF16)


def _gla_mixer_kernel(x_ref, g_ref, w_ref, wup_ref, ba_ref, gh_ref,
                      *refs, n_cast, dk, dv, tiles_per_seq):
    qk, vdim = GLA_HEADS * dk, GLA_HEADS * dv
    cast_src, o_ref, cast_dst = refs[:n_cast], refs[n_cast], refs[n_cast + 1:n_cast + 1 + n_cast]
    scratch = refs[n_cast + 1 + n_cast:]
    st_ref = scratch[-1]
    slice_refs = [scratch[6 * s:6 * s + 6] for s in range((len(scratch) - 1) // 6)]
    _run_cast_jobs(cast_src, cast_dst)
    c = GLA_CHUNK
    tile = x_ref.shape[0]

    @pl.when(pl.program_id(0) % tiles_per_seq == 0)
    def _():
        st_ref[...] = jnp.zeros_like(st_ref)

    per_slice = GLA_PROJ_SLICE // c

    def projection_pieces(s):
        rows = pl.ds(s * GLA_PROJ_SLICE, GLA_PROJ_SLICE)
        hn_s, q_s, k_s, v_s, r_s, la_s = slice_refs[s]

        def norm_and_gate():
            hn = _rmsnorm(x_ref[rows, :], g_ref[...]).astype(BF16)
            hn_s[...] = hn
            a_low = _dot(hn, w_ref[:, 2 * qk + 2 * vdim:])
            z = _dot(a_low.astype(BF16), wup_ref[...]) + ba_ref[...]
            la_s[...] = _log_sigmoid(z) * (1.0 / GLA_TAU)

        def column_block(dst, first_col, j):
            cols = slice(j * MXU_COLS, (j + 1) * MXU_COLS)
            w_cols = slice(first_col + j * MXU_COLS, first_col + (j + 1) * MXU_COLS)

            def run():
                dst[:, cols] = _dot(hn_s[...], w_ref[:, w_cols]).astype(BF16)
            return run

        pieces = [norm_and_gate]
        for dst, first_col, width in ((q_s, 0, qk), (k_s, qk, qk), (v_s, 2 * qk, vdim),
                                      (r_s, 2 * qk + vdim, vdim)):
            pieces += [column_block(dst, first_col, j) for j in range(width // MXU_COLS)]
        return pieces

    causal = (lax.broadcasted_iota(jnp.int32, (c, c), 0)
              >= lax.broadcasted_iota(jnp.int32, (c, c), 1))
    row = lax.broadcasted_iota(jnp.int32, (c, GLA_HEADS * dk), 0)
    g = gh_ref[...]
    scale = dk ** -0.5
    heads = range(GLA_HEADS)
    kcols = [slice(h * dk, (h + 1) * dk) for h in heads]
    vcols = [slice(h * dv, (h + 1) * dv) for h in heads]

    def chunk(n):
        _, q_s, k_s, v_s, r_s, la_s = slice_refs[n // per_slice]
        rows = pl.ds((n % per_slice) * c, c)
        out_rows = pl.ds(n * c, c)
        b = la_s[rows, :]
        shift = 1
        while shift < c:
            b = b + jnp.where(row >= shift, pltpu.roll(b, shift, axis=0), 0.0)
            shift *= 2
        b_last = b[c - 1:c, :]
        yield
        qf = q_s[rows, :].astype(F32) * scale
        kf = k_s[rows, :].astype(F32)
        q_dec = (qf * jnp.exp(b)).astype(BF16)
        k_dec = (kf * jnp.exp(-b)).astype(BF16)
        k_out = (kf * jnp.exp(b_last - b)).astype(BF16)
        decay = jnp.exp(b_last)
        yield
        v = [v_s[rows, vcols[h]] for h in heads]
        a = [lax.dot_general(q_dec[:, kcols[h]], k_dec[:, kcols[h]], _CONTRACT_LAST,
                             preferred_element_type=F32) for h in heads]
        a = [jnp.where(causal, a[h], 0.0).astype(BF16) for h in heads]
        st = [st_ref[h] for h in heads]
        o = [_dot(a[h], v[h]) + lax.dot_general(q_dec[:, kcols[h]], st[h].astype(BF16),
                                                _CONTRACT_LAST, preferred_element_type=F32)
             for h in heads]
        yield
        kv = [lax.dot_general(v[h], k_out[:, kcols[h]], _CONTRACT_FIRST,
                              preferred_element_type=F32) for h in heads]
        for h in heads:
            st_ref[h] = st[h] * decay[:, kcols[h]] + kv[h]
            gate = _silu(r_s[rows, vcols[h]].astype(F32))
            o_ref[out_rows, vcols[h]] = (_rmsnorm(o[h], g) * gate).astype(BF16)

    n_slices = tile // GLA_PROJ_SLICE
    for piece in projection_pieces(0):
        piece()
    for s in range(n_slices):
        ahead = projection_pieces(s + 1) if s + 1 < n_slices else []
        for n in range(s * per_slice, (s + 1) * per_slice):
            stages = chunk(n)
            running = True
            while running:
                if ahead:
                    ahead.pop(0)()
                running = next(stages, "done") != "done"
        assert not ahead


def _gla_mixer(x2d, g, w_in, w_up, b_alpha, g_head, seq, cast_jobs=()):
    m, d = x2d.shape
    tile, sl = GLA_TIME_TILE, GLA_PROJ_SLICE
    steps = m // tile
    jobs = [_cast_job(stack, layer, steps) for stack, layer in cast_jobs]
    qk = w_up.shape[1]
    vdim = (w_in.shape[1] - 2 * qk - GLA_RANK) // 2
    dk, dv = qk // GLA_HEADS, vdim // GLA_HEADS
    assert qk % MXU_COLS == 0 and vdim % MXU_COLS == 0 and w_up.shape[0] == GLA_RANK
    row = lambda n: pl.BlockSpec((tile, n), lambda i: (i, 0))
    return pl.pallas_call(
        functools.partial(_gla_mixer_kernel, n_cast=len(jobs), dk=dk, dv=dv,
                          tiles_per_seq=seq // tile),
        grid=(steps,),
        in_specs=[row(d), _resident((1, d)), _resident(w_in.shape), _resident(w_up.shape),
                  _resident((1, qk)), _resident((1, dv))] + [j[0] for j in jobs],
        out_specs=[row(vdim)] + [j[1] for j in jobs],
        out_shape=[jax.ShapeDtypeStruct((m, vdim), BF16)] + [j[2] for j in jobs],
        scratch_shapes=[pltpu.VMEM((sl, w), t) for _ in range(tile // sl) for w, t in (
            (d, BF16), (qk, BF16), (qk, BF16), (vdim, BF16), (vdim, BF16), (qk, F32))]
        + [pltpu.VMEM((GLA_HEADS, dv, dk), F32)],
        compiler_params=pltpu.CompilerParams(dimension_semantics=("arbitrary",),
                                             vmem_limit_bytes=48 * MIB),
        name="gla_mixer",
    )(x2d, g.reshape(1, d), w_in.astype(BF16), w_up.astype(BF16), b_alpha.reshape(1, qk),
      g_head.reshape(1, dv), *[stack for stack, _ in cast_jobs])


def _out_ffn_kernel(x_ref, o_ref, wo_ref, g_ref, wgu_ref, wd_ref, gf_ref, *refs, final, n_cast):
    cast_src, y_ref, cast_dst = refs[:n_cast], refs[n_cast], refs[n_cast + 1:]
    _run_cast_jobs(cast_src, cast_dst)
    x1 = x_ref[...] + _dot(o_ref[...], wo_ref[...])
    hn = _rmsnorm(x1, g_ref[...]).astype(BF16)
    dff = wd_ref.shape[0]
    fc = dff // FFN_CHUNKS
    acc = x1
    for ci in range(FFN_CHUNKS):
        gate = _dot(hn, wgu_ref[:, ci * fc:(ci + 1) * fc])
        up = _dot(hn, wgu_ref[:, dff + ci * fc:dff + (ci + 1) * fc])
        acc = acc + _dot((_silu(gate) * up).astype(BF16), wd_ref[ci * fc:(ci + 1) * fc, :])
    if final:
        acc = _rmsnorm(acc, gf_ref[...])
    y_ref[...] = acc


def _out_ffn(x2d, o2d, w_o, g_ffn, wgu, wd, g_final, final, cast_jobs=()):
    m, d = x2d.shape
    dff = wd.shape[0]
    assert dff % (FFN_CHUNKS * LANES) == 0
    steps = m // FFN_ROW_TILE
    jobs = [_cast_job(stack, layer, steps) for stack, layer in cast_jobs]
    row = lambda: pl.BlockSpec((FFN_ROW_TILE, d), lambda i: (i, 0))
    return pl.pallas_call(
        functools.partial(_out_ffn_kernel, final=final, n_cast=len(jobs)),
        grid=(steps,),
        in_specs=[row(), row(), _resident((d, d)), _resident((1, d)), _resident((d, 2 * dff)),
                  _resident((dff, d)), _resident((1, d))] + [j[0] for j in jobs],
        out_specs=[row()] + [j[1] for j in jobs],
        out_shape=[jax.ShapeDtypeStruct((m, d), F32)] + [j[2] for j in jobs],
        compiler_params=pltpu.CompilerParams(dimension_semantics=("arbitrary",),
                                             vmem_limit_bytes=57 * MIB),
        name="out_proj_ffn",
    )(x2d, o2d, w_o.astype(BF16), g_ffn.reshape(1, d), wgu, wd, g_final.reshape(1, d),
      *[stack for stack, _ in cast_jobs])


AUG_QSIDE_C = FOX_HEAD_DIM
AUG_KSIDE_C = FOX_HEAD_DIM + 3
AUG_PARTS = 3


def _split3(x):
    hi = x.astype(BF16)
    r1 = x - hi.astype(F32)
    mid = r1.astype(BF16)
    lo = (r1 - mid.astype(F32)).astype(BF16)
    return hi, mid, lo


def _kv_kernel(x_ref, g_ref, wk_ref, wvt_ref, wft_ref, bf_ref, place_ref,
               k_ref, vt_ref, c_ref, carry_ref):
    @pl.when(pl.program_id(1) == 0)
    def _():
        carry_ref[...] = jnp.zeros_like(carry_ref)

    hn = _rmsnorm(x_ref[...], g_ref[...]).astype(BF16)
    k = _dot(hn, wk_ref[...])
    f_t = lax.dot_general(wft_ref[...], hn, _CONTRACT_LAST, preferred_element_type=F32)
    log_f = _log_sigmoid(f_t + bf_ref[...])
    n = log_f.shape[1]
    upper = (lax.broadcasted_iota(jnp.int32, (n, n), 0)
             <= lax.broadcasted_iota(jnp.int32, (n, n), 1)).astype(F32)
    c = jnp.dot(log_f, upper, precision=HIGHEST, preferred_element_type=F32) + carry_ref[:, 0:1]
    carry_ref[...] = jnp.broadcast_to(c[:, n - 1:n], carry_ref.shape)
    c2 = c * LOG2E
    c_ref[...] = c2
    hi, mid, lo = _split3(-c2)
    parts = jnp.concatenate([hi, mid, lo, jnp.ones_like(hi)], axis=0)
    bias_lanes = lax.dot_general(parts, place_ref[...], _CONTRACT_FIRST,
                                 preferred_element_type=F32)
    vt_ref[...] = lax.dot_general(wvt_ref[...], hn, _CONTRACT_LAST,
                                  preferred_element_type=F32).astype(BF16)
    hd = FOX_HEAD_DIM
    low = lax.broadcasted_iota(jnp.int32, (n, LANES), 1) < hd
    for pair in range(k.shape[1] // LANES):
        kp = k[:, pair * LANES:(pair + 1) * LANES]
        for odd, src in enumerate((kp, pltpu.roll(kp, hd, axis=1))):
            cols = slice((2 * pair + odd) * LANES, (2 * pair + odd + 1) * LANES)
            k_ref[:, cols] = jnp.where(low, src, bias_lanes[:, cols]).astype(BF16)


def _kv_placement(heads):
    place = np.zeros((4 * heads, heads * LANES), np.float32)
    h = np.arange(heads)
    for part in range(AUG_PARTS):
        place[part * heads + h, h * LANES + AUG_KSIDE_C + part] = 1.0
        place[AUG_PARTS * heads, h * LANES + AUG_QSIDE_C + part] = 1.0
    return jnp.asarray(place, BF16)


def _shared_kv(x3d, g, w_kv, b_f):
    batch, seq, d = x3d.shape
    heads = b_f.shape[0]
    wk = w_kv[:, :d].astype(BF16)
    wvt = w_kv[:, d:2 * d].T.astype(BF16)
    wft = w_kv[:, 2 * d:].T.astype(BF16)
    kw = heads * LANES
    return pl.pallas_call(
        _kv_kernel,
        grid=(batch, seq // KV_TILE),
        in_specs=[pl.BlockSpec((None, KV_TILE, d), lambda b, t: (b, t, 0)),
                  _resident((1, d)), _resident((d, d)), _resident((d, d)),
                  _resident((heads, d)), _resident((heads, 1)), _resident((4 * heads, kw))],
        out_specs=[pl.BlockSpec((None, KV_TILE, kw), lambda b, t: (b, t, 0)),
                   pl.BlockSpec((None, d, KV_TILE), lambda b, t: (b, 0, t)),
                   pl.BlockSpec((None, heads, KV_TILE), lambda b, t: (b, 0, t))],
        out_shape=[jax.ShapeDtypeStruct((batch, seq, kw), BF16),
                   jax.ShapeDtypeStruct((batch, d, seq), BF16),
                   jax.ShapeDtypeStruct((batch, heads, seq), F32)],
        scratch_shapes=[pltpu.VMEM((heads, LANES), F32)],
        compiler_params=pltpu.CompilerParams(dimension_semantics=("parallel", "arbitrary"),
                                             vmem_limit_bytes=40 * MIB),
        name="shared_kv",
    )(x3d, g.reshape(1, d), wk, wvt, wft, b_f.reshape(heads, 1), _kv_placement(heads))


def _q_kernel(x_ref, g_ref, wqt_ref, c_ref, qt_ref):
    hd = FOX_HEAD_DIM
    hn = _rmsnorm(x_ref[...], g_ref[...]).astype(BF16)
    qt = lax.dot_general(wqt_ref[...], hn, _CONTRACT_LAST, preferred_element_type=F32)
    qt = (qt * (hd ** -0.5 * LOG2E)).astype(BF16)
    hi, mid, lo = (p.astype(F32) for p in _split3(c_ref[...]))
    n = hn.shape[0]
    r = lax.broadcasted_iota(jnp.int32, (BF16_SUBLANES, n), 0)
    for h in range(c_ref.shape[0]):
        base = h * LANES
        bias = jnp.where(r == 0, hi[h:h + 1], jnp.where(r == 1, mid[h:h + 1], jnp.where(
            r == 2, lo[h:h + 1], jnp.where(r < 2 * AUG_PARTS, 1.0, 0.0))))
        qt_ref[base:base + hd, :] = qt[h * hd:(h + 1) * hd, :]
        qt_ref[base + hd:base + hd + BF16_SUBLANES, :] = bias.astype(BF16)
        qt_ref[base + hd + BF16_SUBLANES:base + LANES, :] = jnp.zeros(
            (LANES - hd - BF16_SUBLANES, n), BF16)


def _q_proj(x3d, g, w_q, c2):
    batch, seq, d = x3d.shape
    heads = c2.shape[1]
    return pl.pallas_call(
        _q_kernel,
        grid=(batch, seq // ROW_TILE),
        in_specs=[pl.BlockSpec((None, ROW_TILE, d), lambda b, t: (b, t, 0)),
                  _resident((1, d)), _resident((d, d)),
                  pl.BlockSpec((None, heads, ROW_TILE), lambda b, t: (b, 0, t))],
        out_specs=pl.BlockSpec((None, heads * LANES, ROW_TILE), lambda b, t: (b, 0, t)),
        out_shape=jax.ShapeDtypeStruct((batch, heads * LANES, seq), BF16),
        compiler_params=pltpu.CompilerParams(dimension_semantics=("parallel", "parallel"),
                                             vmem_limit_bytes=40 * MIB),
        name="fox_q_proj",
    )(x3d, g.reshape(1, d), w_q.T.astype(BF16), c2)


def _fox_kernel(qt_ref, k_ref, vt_ref, o_ref, s_ref, m_ref, acc_ref, *, n_q):
    t = FOX_TILE
    causal = (lax.broadcasted_iota(jnp.int32, (t, t), 0)
              <= lax.broadcasted_iota(jnp.int32, (t, t), 1))
    ones = jnp.ones((BF16_SUBLANES, t), BF16)
    for g0 in range(0, FOX_GROUPS_PER_STEP * FOX_HEADS_PER_STEP, FOX_HEADS_PER_STEP):
        _fox_group(qt_ref, k_ref, vt_ref, o_ref, s_ref, m_ref, acc_ref, g0, n_q, causal, ones)


def _fox_group(qt_ref, k_ref, vt_ref, o_ref, s_ref, m_ref, acc_ref, g0, n_q, causal, ones):
    t = FOX_TILE
    hd = FOX_HEAD_DIM
    heads = range(FOX_HEADS_PER_STEP)

    def logits(qi, j, slot):
        keys = pl.ds(j * t, t)
        queries = pl.ds(qi * t, t)
        for h in heads:
            s_ref[slot, h] = _dot(k_ref[keys, (g0 + h) * LANES:(g0 + h + 1) * LANES],
                                  qt_ref[(g0 + h) * LANES:(g0 + h + 1) * LANES, queries])

    def update(qi, j, slot, diagonal):
        keys = pl.ds(j * t, t)

        def tile(h):
            s = s_ref[slot, h]
            return jnp.where(causal, s, NEG_BIG) if diagonal else s

        if diagonal:
            m_new = [jnp.max(tile(h), axis=0, keepdims=True) for h in heads]
        else:
            m_old = [m_ref[qi, h] for h in heads]
            m_new = [jnp.maximum(m_old[h], jnp.max(tile(h), axis=0, keepdims=True)) for h in heads]
            alpha = [jnp.exp2(m_old[h] - m_new[h]) for h in heads]
        p = [jnp.exp2(tile(h) - m_new[h]).astype(BF16) for h in heads]
        pv = [_dot(jnp.concatenate([vt_ref[(g0 + h) * hd:(g0 + h + 1) * hd, keys], ones], axis=0),
                   p[h]) for h in heads]
        for h in heads:
            m_ref[qi, h] = m_new[h]
            acc_ref[qi, h] = pv[h] if diagonal else alpha[h] * acc_ref[qi, h] + pv[h]

    ns, ahead = FOX_SLOTS, FOX_LOOKAHEAD
    tiles = [(qi, qi) for qi in range(n_q)]
    tiles += [(qi, j) for j in range(n_q) for qi in range(j + 1, n_q)]
    for w in range(ahead):
        logits(*tiles[w], w)
    for k, (qi, j) in enumerate(tiles):
        if k + ahead < len(tiles):
            logits(*tiles[k + ahead], (k + ahead) % ns)
        update(qi, j, k % ns, qi == j)

    for qi in range(n_q):
        o_t = jnp.concatenate([acc_ref[qi, h, :hd] / acc_ref[qi, h, hd:hd + 1] for h in heads],
                              axis=0)
        o_ref[pl.ds(qi * t, t), g0 * hd:(g0 + FOX_HEADS_PER_STEP) * hd] = o_t.T.astype(BF16)


def _fox_attention(qt, k_aug, vt, batch, seq, d):
    nh = FOX_HEADS_PER_STEP
    nb = nh * FOX_GROUPS_PER_STEP
    groups = d // (nb * FOX_HEAD_DIM)
    t = FOX_TILE
    out = pl.pallas_call(
        functools.partial(_fox_kernel, n_q=seq // t),
        grid=(batch, groups),
        in_specs=[pl.BlockSpec((None, nb * LANES, seq), lambda b, p: (b, p, 0)),
                  pl.BlockSpec((None, seq, nb * LANES), lambda b, p: (b, 0, p)),
                  pl.BlockSpec((None, nb * FOX_HEAD_DIM, seq), lambda b, p: (b, p, 0))],
        out_specs=pl.BlockSpec((None, seq, nb * FOX_HEAD_DIM), lambda b, p: (b, 0, p)),
        out_shape=jax.ShapeDtypeStruct((batch, seq, d), BF16),
        scratch_shapes=[pltpu.VMEM((FOX_SLOTS, nh, t, t), F32),
                        pltpu.VMEM((seq // t, nh, 1, t), F32),
                        pltpu.VMEM((seq // t, nh, FOX_HEAD_DIM + BF16_SUBLANES, t), F32)],
        compiler_params=pltpu.CompilerParams(
            dimension_semantics=("parallel", "parallel"),
            vmem_limit_bytes=40 * MIB),
        name="fox_attention",
    )(qt, k_aug, vt)
    return out.reshape(batch * seq, d)


def kernel(x, attn_norm, ffn_norm, gla_w_in, gla_w_alpha_up, gla_b_alpha, gla_g_head, gla_w_out,
           kv_norm, w_kv, b_f, fox_w_q, fox_w_o, ffn_w_gu, ffn_w_down, final_norm):
    batch, seq, d = x.shape
    depth = attn_norm.shape[0]
    n_gla = gla_w_in.shape[0]
    assert seq % GLA_TIME_TILE == 0 and seq % ROW_TILE == 0
    assert GLA_TIME_TILE % GLA_PROJ_SLICE == 0 and GLA_PROJ_SLICE % GLA_CHUNK == 0
    assert seq % KV_TILE == 0 and seq % FOX_TILE == 0
    assert (FOX_HEADS_PER_STEP * FOX_HEAD_DIM) % LANES == 0
    assert d // b_f.shape[0] == FOX_HEAD_DIM

    h = x.reshape(batch * seq, d)
    k_aug = v_t = c2 = None
    ffn_w = None
    for layer in range(depth):
        if layer < n_gla:
            jobs = [(ffn_w_gu, 0), (ffn_w_down, 0)] if layer == 0 else []
            o, *cast = _gla_mixer(h, attn_norm[layer], gla_w_in[layer], gla_w_alpha_up[layer],
                                  gla_b_alpha[layer], gla_g_head[layer], seq, jobs)
            ffn_w = cast or ffn_w
            w_o = gla_w_out[layer]
        else:
            h3 = h.reshape(batch, seq, d)
            if layer == n_gla:
                k_aug, v_t, c2 = _shared_kv(h3, kv_norm, w_kv, b_f)
            j = layer - n_gla
            q_t = _q_proj(h3, attn_norm[layer], fox_w_q[j], c2)
            o = _fox_attention(q_t, k_aug, v_t, batch, seq, d)
            w_o = fox_w_o[j]
        if ffn_w is None:
            ffn_w = [ffn_w_gu[layer].astype(BF16), ffn_w_down[layer].astype(BF16)]
        jobs = [(ffn_w_gu, layer + 1), (ffn_w_down, layer + 1)] if layer + 1 < depth else []
        h, *cast = _out_ffn(h, o, w_o, ffn_norm[layer], ffn_w[0], ffn_w[1], final_norm,
                            final=(layer == depth - 1), cast_jobs=jobs)
        ffn_w = cast or None
    return h.reshape(batch, seq, d)
```
